```python
import math
import jax, jax.numpy as jnp
from jax import lax
import numpy as np

D_MODEL = 1024
BATCH = 16
SEQ = 2048
DEPTH = 1

GRID_W = 64
MEM_LEN = 256
BRANCH_W = D_MODEL // 2
N_BRANCH = 3
DN_HEADS = 4
DN_HEAD_DIM = BRANCH_W // DN_HEADS
DN_CHUNK = 64
CONV_W = 5
CONV_PAD = CONV_W // 2
ATT_HEADS = 8
ATT_KV_HEADS = 2
ATT_GROUP = ATT_HEADS // ATT_KV_HEADS
ATT_HEAD_DIM = BRANCH_W // ATT_HEADS
ROPE_AXIS_DIM = ATT_HEAD_DIM // 2
ROPE_PAIRS = ROPE_AXIS_DIM // 2
ROPE_THETA = 10000.0
Q_BLOCK = 128
X_HEADS = 4
X_HEAD_DIM = BRANCH_W // X_HEADS
D_FF = 4 * D_MODEL
EPS = 1e-6

DN_QKV_COLS = 3 * BRANCH_W
DN_Z_COLS = BRANCH_W
DN_BETA_COLS = 2 * DN_HEADS
DN_ALPHA_COLS = 2 * DN_HEADS
ATT_Q_COLS = ATT_HEADS * ATT_HEAD_DIM
ATT_KV_COLS = ATT_KV_HEADS * ATT_HEAD_DIM
X_Q_COLS = X_HEADS * X_HEAD_DIM
GATE_COLS = N_BRANCH * D_MODEL
IN_SPLITS = (DN_QKV_COLS, DN_Z_COLS, DN_BETA_COLS, DN_ALPHA_COLS, ATT_Q_COLS, ATT_KV_COLS, ATT_KV_COLS, X_Q_COLS, GATE_COLS)
IN_COLS = sum(IN_SPLITS)
IN_OFFSETS = tuple(int(o) for o in np.cumsum(IN_SPLITS)[:-1])

kernel_name = "hybrid_gdn_axial_gqa_memxattn_block"


def rms_norm(x, w):
    xf = x.astype(jnp.float32)
    y = xf * lax.rsqrt(jnp.mean(xf * xf, axis=-1, keepdims=True) + EPS)
    return (y * w.astype(jnp.float32)).astype(x.dtype)


def l2_normalize(x):
    xf = x.astype(jnp.float32)
    return xf * lax.rsqrt(jnp.sum(xf * xf, axis=-1, keepdims=True) + EPS)


def centred_depthwise_conv(x, w):
    c = x.shape[-1]
    return lax.conv_general_dilated(
        x, w[:, None, :].astype(x.dtype), window_strides=(1,), padding=[(CONV_PAD, CONV_PAD)],
        dimension_numbers=("NWC", "WIO", "NWC"), feature_group_count=c)


def gated_delta_rule(q, k, v, g, beta):
    out_dtype = v.dtype
    q, k, v, g, beta = (t.astype(jnp.float32) for t in (q, k, v, g, beta))
    b_, s_, h_, dk = q.shape
    dv = v.shape[-1]
    n_ = s_ // DN_CHUNK
    q = q * (dk ** -0.5)

    def chunks(t):
        return t.reshape(b_, n_, DN_CHUNK, h_, t.shape[-1]).transpose(0, 3, 1, 2, 4)

    qc, kc, vc = chunks(q), chunks(k), chunks(v)
    bc = beta.reshape(b_, n_, DN_CHUNK, h_).transpose(0, 3, 1, 2)
    gc = jnp.cumsum(g.reshape(b_, n_, DN_CHUNK, h_).transpose(0, 3, 1, 2), axis=-1)
    idx = jnp.arange(DN_CHUNK)
    lower_incl = idx[:, None] >= idx[None, :]
    strict = idx[:, None] > idx[None, :]
    decay = jnp.exp(jnp.where(lower_incl, gc[..., :, None] - gc[..., None, :], -jnp.inf))
    kb = kc * bc[..., None]
    vb = vc * bc[..., None]
    lmat = jnp.where(strict, jnp.einsum("bhncd,bhnsd->bhncs", kb, kc) * decay, 0.0)
    rhs = jnp.concatenate([vb, kb * jnp.exp(gc)[..., None]], axis=-1)
    sol = lax.linalg.triangular_solve(lmat, rhs, left_side=True, lower=True, unit_diagonal=True)
    u, w = sol[..., :dv], sol[..., dv:]
    attn_intra = jnp.einsum("bhncd,bhnsd->bhncs", qc, kc) * decay
    q_dec = qc * jnp.exp(gc)[..., None]
    g_last = gc[..., -1]
    k_dec = kc * jnp.exp(g_last[..., None] - gc)[..., None]

    def step(state, inp):
        u_i, w_i, q_i, k_i, a_i, gl = inp
        v_new = u_i - jnp.einsum("bhcd,bhdv->bhcv", w_i, state)
        o = jnp.einsum("bhcd,bhdv->bhcv", q_i, state) + jnp.einsum("bhcs,bhsv->bhcv", a_i, v_new)
        state = state * jnp.exp(gl)[..., None, None] + jnp.einsum("bhcd,bhcv->bhdv", k_i, v_new)
        return state, o

    xs = (jnp.moveaxis(u, 2, 0), jnp.moveaxis(w, 2, 0), jnp.moveaxis(q_dec, 2, 0),
          jnp.moveaxis(k_dec, 2, 0), jnp.moveaxis(attn_intra, 2, 0), jnp.moveaxis(g_last, 2, 0))
    state0 = jnp.zeros((b_, h_, dk, dv), jnp.float32)
    _, o = lax.scan(step, state0, xs)
    return o.transpose(1, 0, 3, 2, 4).reshape(b_, s_, h_, dv).astype(out_dtype)


def axial_rope_tables(row_ids, col_ids):
    inv_freq = ROPE_THETA ** (-jnp.arange(ROPE_PAIRS, dtype=jnp.float32) / ROPE_PAIRS)
    ang = jnp.stack([row_ids.astype(jnp.float32)[:, None] * inv_freq,
                     col_ids.astype(jnp.float32)[:, None] * inv_freq], axis=1)
    return jnp.cos(ang), jnp.sin(ang)


def apply_axial_rope(x, cos, sin):
    xs = x.astype(jnp.float32).reshape(*x.shape[:-1], 2, 2, ROPE_PAIRS)
    x1, x2 = xs[..., 0, :], xs[..., 1, :]
    c = cos[None, :, None]
    s = sin[None, :, None]
    out = jnp.stack([x1 * c - x2 * s, x2 * c + x1 * s], axis=-2)
    return out.reshape(x.shape).astype(x.dtype)


def block_gqa(q, k, v):
    b_, s_ = q.shape[:2]
    nb = s_ // Q_BLOCK
    qb = q.reshape(b_, nb, Q_BLOCK, ATT_KV_HEADS, ATT_GROUP, ATT_HEAD_DIM).transpose(1, 0, 2, 3, 4, 5)
    scale = ATT_HEAD_DIM ** -0.5

    def one_block(qi):
        sc = jnp.einsum("bqkgd,bskd->bkgqs", qi, k).astype(jnp.float32) * scale
        p = jax.nn.softmax(sc, axis=-1).astype(v.dtype)
        return jnp.einsum("bkgqs,bskd->bqkgd", p, v)

    o = lax.map(one_block, qb)
    return o.transpose(1, 0, 2, 3, 4, 5).reshape(b_, s_, ATT_HEADS * ATT_HEAD_DIM)


def cross_attention(q, mk, mv):
    sc = jnp.einsum("bshd,bmhd->bhsm", q, mk).astype(jnp.float32) * (X_HEAD_DIM ** -0.5)
    p = jax.nn.softmax(sc, axis=-1).astype(mv.dtype)
    o = jnp.einsum("bhsm,bmhd->bshd", p, mv)
    return o.reshape(q.shape[0], q.shape[1], X_HEADS * X_HEAD_DIM)


def setup_inputs(seed: int = 0) -> dict:
    key = jax.random.key(seed)
    ks = jax.random.split(key, 20)
    f32 = jnp.float32

    def normal(k, shape, scale):
        return jax.random.normal(k, shape, f32) * scale

    def gain(k, shape):
        return 1.0 + 0.02 * jax.random.normal(k, shape, f32)

    x = normal(ks[0], (BATCH, SEQ, D_MODEL), 1.0)
    mem = normal(ks[1], (BATCH, MEM_LEN, D_MODEL), 1.0)
    mix_norm_w = gain(ks[2], (DEPTH, D_MODEL))
    w_in = normal(ks[3], (DEPTH, D_MODEL, IN_COLS), D_MODEL ** -0.5)
    dn_conv_w = normal(ks[4], (DEPTH, CONV_W, DN_QKV_COLS), CONV_W ** -0.5)
    dn_a_log = jnp.log(jax.random.uniform(ks[5], (DEPTH, 2, DN_HEADS), f32, 1.0, 16.0))
    dt = jnp.exp(jax.random.uniform(ks[6], (DEPTH, 2, DN_HEADS), f32, math.log(1e-3), math.log(1e-1)))
    dn_dt_bias = dt + jnp.log(-jnp.expm1(-dt))
    dn_norm_w = gain(ks[7], (DEPTH, DN_HEAD_DIM))
    q_norm_w = gain(ks[8], (DEPTH, ATT_HEAD_DIM))
    k_norm_w = gain(ks[9], (DEPTH, ATT_HEAD_DIM))
    mem_norm_w = gain(ks[10], (DEPTH, D_MODEL))
    w_mem_kv = normal(ks[11], (DEPTH, D_MODEL, 2 * X_HEADS * X_HEAD_DIM), D_MODEL ** -0.5)
    w_branch = normal(ks[12], (DEPTH, N_BRANCH, BRANCH_W, D_MODEL), BRANCH_W ** -0.5)
    w_out = normal(ks[13], (DEPTH, D_MODEL, D_MODEL), D_MODEL ** -0.5)
    ffn_norm_w = gain(ks[14], (DEPTH, D_MODEL))
    w_up = normal(ks[15], (DEPTH, D_MODEL, D_FF), D_MODEL ** -0.5)
    w_down = normal(ks[16], (DEPTH, D_FF, D_MODEL), D_FF ** -0.5)
    final_norm_w = gain(ks[17], (D_MODEL,))
    return {"x": x, "mem": mem, "mix_norm_w": mix_norm_w, "w_in": w_in, "dn_conv_w": dn_conv_w,
            "dn_a_log": dn_a_log, "dn_dt_bias": dn_dt_bias, "dn_norm_w": dn_norm_w,
            "q_norm_w": q_norm_w, "k_norm_w": k_norm_w, "mem_norm_w": mem_norm_w, "w_mem_kv": w_mem_kv,
            "w_branch": w_branch, "w_out": w_out, "ffn_norm_w": ffn_norm_w, "w_up": w_up,
            "w_down": w_down, "final_norm_w": final_norm_w}


def reference(x, mem, mix_norm_w, w_in, dn_conv_w, dn_a_log, dn_dt_bias, dn_norm_w, q_norm_w, k_norm_w,
              mem_norm_w, w_mem_kv, w_branch, w_out, ffn_norm_w, w_up, w_down, final_norm_w):
    b_, s_, _ = x.shape
    m_ = mem.shape[1]
    rows = s_ // GRID_W
    row_ids = jnp.repeat(jnp.arange(rows), GRID_W)
    col_ids = jnp.tile(jnp.arange(GRID_W), rows)
    cos, sin = axial_rope_tables(row_ids, col_ids)

    def flip(t):
        return jnp.flip(t, axis=1)

    for l in range(DEPTH):
        h = rms_norm(x, mix_norm_w[l])
        proj = h @ w_in[l]
        dn_qkv, dn_z, dn_b, dn_a, a_q, a_k, a_v, x_q, gate_logits = jnp.split(proj, IN_OFFSETS, axis=-1)

        dn_qkv = jax.nn.silu(centred_depthwise_conv(dn_qkv, dn_conv_w[l]))
        dq, dk, dv = jnp.split(dn_qkv, 3, axis=-1)
        dq = l2_normalize(dq.reshape(b_, s_, DN_HEADS, DN_HEAD_DIM))
        dk = l2_normalize(dk.reshape(b_, s_, DN_HEADS, DN_HEAD_DIM))
        dv = dv.reshape(b_, s_, DN_HEADS, DN_HEAD_DIM)
        beta = jax.nn.sigmoid(dn_b.astype(jnp.float32)).reshape(b_, s_, 2, DN_HEADS)
        g = -jnp.exp(dn_a_log[l].astype(jnp.float32)) * jax.nn.softplus(
            dn_a.astype(jnp.float32).reshape(b_, s_, 2, DN_HEADS) + dn_dt_bias[l].astype(jnp.float32))
        o_fwd = gated_delta_rule(dq, dk, dv, g[:, :, 0], beta[:, :, 0])
        o_bwd = flip(gated_delta_rule(flip(dq), flip(dk), flip(dv), flip(g[:, :, 1]), flip(beta[:, :, 1])))
        o_dn = rms_norm(o_fwd + o_bwd, dn_norm_w[l]) * jax.nn.silu(dn_z.reshape(b_, s_, DN_HEADS, DN_HEAD_DIM))
        y_dn = o_dn.reshape(b_, s_, BRANCH_W)

        aq = rms_norm(a_q.reshape(b_, s_, ATT_HEADS, ATT_HEAD_DIM), q_norm_w[l])
        ak = rms_norm(a_k.reshape(b_, s_, ATT_KV_HEADS, ATT_HEAD_DIM), k_norm_w[l])
        av = a_v.reshape(b_, s_, ATT_KV_HEADS, ATT_HEAD_DIM)
        aq = apply_axial_rope(aq, cos, sin)
        ak = apply_axial_rope(ak, cos, sin)
        y_att = block_gqa(aq, ak, av)

        mkv = rms_norm(mem, mem_norm_w[l]) @ w_mem_kv[l]
        mk, mv = jnp.split(mkv, 2, axis=-1)
        y_x = cross_attention(x_q.reshape(b_, s_, X_HEADS, X_HEAD_DIM),
                              mk.reshape(b_, m_, X_HEADS, X_HEAD_DIM),
                              mv.reshape(b_, m_, X_HEADS, X_HEAD_DIM))

        ys = jnp.stack([y_dn, y_att, y_x], axis=2)
        yb = jnp.einsum("bsnc,ncd->bsnd", ys, w_branch[l])
        gates = jax.nn.sigmoid(gate_logits.reshape(b_, s_, N_BRANCH, D_MODEL))
        merged = jnp.sum(gates * yb, axis=2)
        x = x + merged @ w_out[l]

        hf = rms_norm(x, ffn_norm_w[l])
        x = x + jnp.square(jax.nn.relu(hf @ w_up[l])) @ w_down[l]

    return rms_norm(x, final_norm_w)
```

```python
import functools

import jax
import jax.numpy as jnp
from jax import lax
from jax.experimental import pallas as pl
from jax.experimental.pallas import tpu as pltpu

F32 = jnp.float32
BF16 = jnp.bfloat16

DN_HEADS = 4
DN_HEAD_DIM = 128
CONV_W = 5
ATT_HEADS = 8
ATT_KV_HEADS = 2
ATT_HEAD_DIM = 64
ROPE_PAIRS = ATT_HEAD_DIM // 4
ROPE_THETA = 10000.0
GRID_W = 64
X_HEADS = 4
X_HEAD_DIM = 128
N_BRANCH = 3
EPS = 1e-6

LANES = 128
V7X_VMEM_LIMIT = 56 * 1024 * 1024

DN_CHUNK = 64
DN_BLOCK = 256
DN_HEADS_PER_STEP = 2
DN_GROUPS = DN_HEADS // DN_HEADS_PER_STEP
DN_ALPHA_LANE = 8

NT_DIMS = (((1,), (1,)), ((), ()))
TN_DIMS = (((0,), (0,)), ((), ()))
NEG_BIG = -1e30


def _rms(x, w):
    return (x * lax.rsqrt(jnp.mean(x * x, axis=-1, keepdims=True) + EPS)) * w


def _sigmoid(x):
    return 1.0 / (1.0 + jnp.exp(-x))


def _softplus(x):
    return jnp.maximum(x, 0.0) + jnp.log(1.0 + jnp.exp(-jnp.abs(x)))


def _params(semantics):
    return pltpu.CompilerParams(dimension_semantics=semantics, vmem_limit_bytes=V7X_VMEM_LIMIT)


def _inproj_kernel(x_ref, nw_ref, w_ref, proj_ref, ba_ref, *, n_proj, col_chunk):
    h = _rms(x_ref[...], nw_ref[...]).astype(BF16)
    for c0 in range(0, n_proj, col_chunk):
        c1 = min(c0 + col_chunk, n_proj)
        proj_ref[:, c0:c1] = jnp.dot(h, w_ref[:, c0:c1], preferred_element_type=F32).astype(BF16)
    ba_ref[...] = jnp.dot(h, w_ref[:, n_proj:], preferred_element_type=F32)


def _inproj(x2, norm_w, w_prep, n_proj, tm):
    t, d = x2.shape
    n_all = w_prep.shape[1]
    return pl.pallas_call(
        functools.partial(_inproj_kernel, n_proj=n_proj, col_chunk=512),
        grid=(t // tm,),
        in_specs=[
            pl.BlockSpec((tm, d), lambda i: (i, 0)),
            pl.BlockSpec((1, d), lambda i: (0, 0)),
            pl.BlockSpec((d, n_all), lambda i: (0, 0)),
        ],
        out_specs=[
            pl.BlockSpec((tm, n_proj), lambda i: (i, 0)),
            pl.BlockSpec((tm, n_all - n_proj), lambda i: (i, 0)),
        ],
        out_shape=[
            jax.ShapeDtypeStruct((t, n_proj), BF16),
            jax.ShapeDtypeStruct((t, n_all - n_proj), F32),
        ],
        compiler_params=_params(("arbitrary",)),
        name="inproj",
    )(x2, norm_w, w_prep)


def _deltanet_kernel(dq_ref, dk_ref, dv_ref, z_ref, ba_ref, wq_ref, wk_ref, wv_ref, gpar_ref, nw_ref,
                     y_ref, col_s, row_s, q_s, k_s, v_s, o_s, st_s, mask_s, *, seq, hp):
    c = DN_CHUNK
    blk = DN_BLOCK
    n_blk = seq // blk
    per_blk = blk // c
    hd = DN_HEAD_DIM

    ba = ba_ref[...]
    beta = _sigmoid(ba)
    g = -jnp.exp(gpar_ref[0:1, :]) * _softplus(ba + gpar_ref[1:2, :])
    rin = lax.broadcasted_iota(jnp.int32, (seq, LANES), 0) & (c - 1)
    pre = g
    suf = g
    sh = 1
    while sh < c:
        pre = pre + jnp.where(rin >= sh, pltpu.roll(pre, sh, 0), 0.0)
        suf = suf + jnp.where(rin < c - sh, pltpu.roll(suf, seq - sh, 0), 0.0)
        sh *= 2
    col_s[0] = beta
    col_s[1] = pre
    col_s[2] = suf
    col_s[3] = pre + suf - g
    row_s[0] = pre.T
    row_s[1] = suf.T

    ii = lax.broadcasted_iota(jnp.int32, (blk, blk), 0)
    jj = lax.broadcasted_iota(jnp.int32, (blk, blk), 1)
    same = (ii // c) == (jj // c)
    mask_s[0] = jnp.where(jnp.logical_and(same, ii >= jj), 0.0, NEG_BIG)
    mask_s[1] = jnp.where(jnp.logical_and(same, ii <= jj), 0.0, NEG_BIG)
    mask_s[2] = jnp.where(ii != jj, 1.0, 0.0)

    tpos = lax.broadcasted_iota(jnp.int32, (seq, hd), 0)

    def conv_silu(x_ref, w_ref, hh):
        x = x_ref[:, hh * hd:(hh + 1) * hd].astype(F32)
        w = w_ref[:, hh * hd:(hh + 1) * hd]
        y = x * w[2:3, :]
        y = y + jnp.where(tpos >= 2, pltpu.roll(x, 2, 0), 0.0) * w[0:1, :]
        y = y + jnp.where(tpos >= 1, pltpu.roll(x, 1, 0), 0.0) * w[1:2, :]
        y = y + jnp.where(tpos < seq - 1, pltpu.roll(x, seq - 1, 0), 0.0) * w[3:4, :]
        y = y + jnp.where(tpos < seq - 2, pltpu.roll(x, seq - 2, 0), 0.0) * w[4:5, :]
        return y * _sigmoid(y)

    def l2n(x):
        return x * lax.rsqrt(jnp.sum(x * x, axis=-1, keepdims=True) + EPS)

    for hh in range(hp):
        q_s[hh] = l2n(conv_silu(dq_ref, wq_ref, hh)) * (hd ** -0.5)
        k_s[hh] = l2n(conv_silu(dk_ref, wk_ref, hh))
        v_s[hh] = conv_silu(dv_ref, wv_ref, hh)
    o_s[...] = jnp.zeros(o_s.shape, F32)
    st_s[...] = jnp.zeros(st_s.shape, F32)

    n_double = c.bit_length() - 2

    def block_step(r, carry):
        for hh in range(hp):
            for d in range(2):
                b_i = r if d == 0 else n_blk - 1 - r
                r0 = pl.multiple_of(b_i * blk, blk)
                lane_b = d * hp + hh
                lane_g = DN_ALPHA_LANE + d * hp + hh
                q = q_s[hh, pl.ds(r0, blk), :]
                k = k_s[hh, pl.ds(r0, blk), :]
                v = v_s[hh, pl.ds(r0, blk), :]
                bcol = col_s[0, pl.ds(r0, blk), :][:, lane_b:lane_b + 1]
                gcol = col_s[1 + d, pl.ds(r0, blk), :][:, lane_g:lane_g + 1]
                tcol = col_s[3, pl.ds(r0, blk), :][:, lane_g:lane_g + 1]
                grow = row_s[d, lane_g:lane_g + 1, pl.ds(r0, blk)]
                dec = jnp.exp((gcol - grow) + mask_s[d])
                kb = k * bcol
                k16 = k.astype(BF16)
                kk = lax.dot_general(kb.astype(BF16), k16, NT_DIMS, preferred_element_type=F32)
                qk = lax.dot_general(q.astype(BF16), k16, NT_DIMS, preferred_element_type=F32)
                lm = kk * dec * mask_s[2]
                attn = (qk * dec).astype(BF16)

                x = -lm
                l16 = lm.astype(BF16)
                m = jnp.dot(l16, l16, preferred_element_type=F32)
                for it in range(n_double):
                    m16 = m.astype(BF16)
                    if it < n_double - 1:
                        both = jnp.dot(jnp.concatenate([x.astype(BF16), m16], axis=0), m16,
                                       preferred_element_type=F32)
                        x = x + m + both[:blk]
                        m = both[blk:]
                    else:
                        x = x + m + jnp.dot(x.astype(BF16), m16, preferred_element_type=F32)

                eg = jnp.exp(gcol)
                rhs = jnp.concatenate([v * bcol, kb * eg], axis=1)
                uw = rhs + jnp.dot(x.astype(BF16), rhs.astype(BF16), preferred_element_type=F32)
                u = uw[:, :hd]
                w16 = uw[:, hd:].astype(BF16)
                qd16 = (q * eg).astype(BF16)
                kd16 = (k * jnp.exp(tcol - gcol)).astype(BF16)

                chain = hh * 2 + d
                state = st_s[chain]
                order = range(per_blk) if d == 0 else range(per_blk - 1, -1, -1)
                for ci in order:
                    lo, hi = ci * c, (ci + 1) * c
                    lhs = jnp.concatenate([w16[lo:hi], qd16[lo:hi]], axis=0)
                    ws = jnp.dot(lhs, state.astype(BF16), preferred_element_type=F32)
                    vn16 = (u[lo:hi] - ws[:c]).astype(BF16)
                    o = ws[c:] + jnp.dot(attn[lo:hi, lo:hi], vn16, preferred_element_type=F32)
                    kv = lax.dot_general(kd16[lo:hi], vn16, TN_DIMS, preferred_element_type=F32)
                    state = state * jnp.exp(tcol[lo:lo + 1, :]) + kv
                    rows = pl.ds(r0 + lo, c)
                    o_s[hh, rows, :] = o_s[hh, rows, :] + o
                st_s[chain] = state
        return carry

    lax.fori_loop(0, n_blk, block_step, 0)

    for hh in range(hp):
        zz = z_ref[:, hh * hd:(hh + 1) * hd].astype(F32)
        y_ref[:, hh * hd:(hh + 1) * hd] = (_rms(o_s[hh], nw_ref[...]) * (zz * _sigmoid(zz))).astype(BF16)


def _deltanet(proj, ba, conv_w, gpar, norm_w, batch, seq, col0):
    hp = DN_HEADS_PER_STEP
    wblk = hp * DN_HEAD_DIM
    base = col0 * LANES // wblk
    per = DN_HEADS // hp
    t = batch * seq
    n_chain = 2 * hp
    return pl.pallas_call(
        functools.partial(_deltanet_kernel, seq=seq, hp=hp),
        grid=(batch, per),
        in_specs=[
            pl.BlockSpec((seq, wblk), lambda b, g: (b, base + g)),
            pl.BlockSpec((seq, wblk), lambda b, g: (b, base + per + g)),
            pl.BlockSpec((seq, wblk), lambda b, g: (b, base + 2 * per + g)),
            pl.BlockSpec((seq, wblk), lambda b, g: (b, base + 3 * per + g)),
            pl.BlockSpec((seq, LANES), lambda b, g: (b, g)),
            pl.BlockSpec((CONV_W, wblk), lambda b, g: (0, g)),
            pl.BlockSpec((CONV_W, wblk), lambda b, g: (0, per + g)),
            pl.BlockSpec((CONV_W, wblk), lambda b, g: (0, 2 * per + g)),
            pl.BlockSpec((None, 8, LANES), lambda b, g: (g, 0, 0)),
            pl.BlockSpec((1, DN_HEAD_DIM), lambda b, g: (0, 0)),
        ],
        out_specs=pl.BlockSpec((seq, wblk), lambda b, g: (b, g)),
        out_shape=jax.ShapeDtypeStruct((t, DN_HEADS * DN_HEAD_DIM), BF16),
        scratch_shapes=[
            pltpu.VMEM((4, seq, LANES), F32),
            pltpu.VMEM((2, LANES, seq), F32),
            pltpu.VMEM((hp, seq, DN_HEAD_DIM), F32),
            pltpu.VMEM((hp, seq, DN_HEAD_DIM), F32),
            pltpu.VMEM((hp, seq, DN_HEAD_DIM), F32),
            pltpu.VMEM((hp, seq, DN_HEAD_DIM), F32),
            pltpu.VMEM((n_chain, DN_HEAD_DIM, DN_HEAD_DIM), F32),
            pltpu.VMEM((3, DN_BLOCK, DN_BLOCK), F32),
        ],
        compiler_params=_params(("arbitrary", "arbitrary")),
        name="deltanet",
    )(proj, proj, proj, proj, ba, conv_w, conv_w, conv_w, gpar, norm_w)


def _attn_kernel(q_ref, k_ref, v_ref, cos_ref, sin_ref, qw_ref, kw_ref, o_ref,
                 klo_s, khi_s, vlo_s, vhi_s, *, tq):
    g = pl.program_id(1)
    qi = pl.program_id(2)
    lane = lax.broadcasted_iota(jnp.int32, (1, LANES), 1)
    low = lane < ATT_HEAD_DIM
    first = (lane & (2 * ROPE_PAIRS - 1)) < ROPE_PAIRS

    def norm_rope(x, w, cos, sin):
        sq = x * x
        ms_lo = jnp.sum(jnp.where(low, sq, 0.0), axis=-1, keepdims=True)
        ms_hi = jnp.sum(jnp.where(low, 0.0, sq), axis=-1, keepdims=True)
        ms = jnp.where(low, ms_lo, ms_hi) * (1.0 / ATT_HEAD_DIM)
        y = (x * lax.rsqrt(ms + EPS)) * w
        rot = jnp.where(first, -pltpu.roll(y, LANES - ROPE_PAIRS, 1), pltpu.roll(y, ROPE_PAIRS, 1))
        return y * cos + rot * sin

    @pl.when(qi == 0)
    def _():
        in_g = (lane >= ATT_HEAD_DIM).astype(jnp.int32) == g
        is0 = g == 0
        kr = norm_rope(k_ref[...].astype(F32), kw_ref[...], cos_ref[...], sin_ref[...])
        for src, lo_s, hi_s in ((kr, klo_s, khi_s), (v_ref[...].astype(F32), vlo_s, vhi_s)):
            sel = jnp.where(in_g, src, 0.0)
            oth = pltpu.roll(sel, ATT_HEAD_DIM, 1)
            lo_s[...] = jnp.where(is0, sel, oth).astype(BF16)
            hi_s[...] = jnp.where(is0, oth, sel).astype(BF16)

    r0 = pl.multiple_of(qi * tq, tq)
    cos_q = cos_ref[pl.ds(r0, tq), :]
    sin_q = sin_ref[pl.ds(r0, tq), :]
    for jb in range(q_ref.shape[1] // LANES):
        qb = norm_rope(q_ref[:, jb * LANES:(jb + 1) * LANES].astype(F32), qw_ref[...], cos_q, sin_q)
        qb = (qb * (ATT_HEAD_DIM ** -0.5)).astype(BF16)
        acc = None
        for k_s, v_s in ((klo_s, vlo_s), (khi_s, vhi_s)):
            s = lax.dot_general(qb, k_s[...], NT_DIMS, preferred_element_type=F32)
            p = jnp.exp(s - jnp.max(s, axis=-1, keepdims=True))
            den = jnp.sum(p, axis=-1, keepdims=True)
            o = jnp.dot(p.astype(BF16), v_s[...], preferred_element_type=F32) / den
            acc = o if acc is None else acc + o
        o_ref[:, jb * LANES:(jb + 1) * LANES] = acc.astype(BF16)


def _attention(proj, cos_t, sin_t, qw, kw, batch, seq, q_col0, k_col, v_col, tq):
    t = batch * seq
    q_w = ATT_HEADS * ATT_HEAD_DIM // ATT_KV_HEADS
    q_base = q_col0 * LANES // q_w
    n_q = seq // tq
    return pl.pallas_call(
        functools.partial(_attn_kernel, tq=tq),
        grid=(batch, ATT_KV_HEADS, n_q),
        in_specs=[
            pl.BlockSpec((tq, q_w), lambda b, g, i: (b * n_q + i, q_base + g)),
            pl.BlockSpec((seq, LANES), lambda b, g, i: (b, k_col)),
            pl.BlockSpec((seq, LANES), lambda b, g, i: (b, v_col)),
            pl.BlockSpec((seq, LANES), lambda b, g, i: (0, 0)),
            pl.BlockSpec((seq, LANES), lambda b, g, i: (0, 0)),
            pl.BlockSpec((1, LANES), lambda b, g, i: (0, 0)),
            pl.BlockSpec((1, LANES), lambda b, g, i: (0, 0)),
        ],
        out_specs=pl.BlockSpec((tq, q_w), lambda b, g, i: (b * n_q + i, g)),
        out_shape=jax.ShapeDtypeStruct((t, ATT_HEADS * ATT_HEAD_DIM), BF16),
        scratch_shapes=[pltpu.VMEM((seq, LANES), BF16) for _ in range(4)],
        compiler_params=_params(("arbitrary", "arbitrary", "arbitrary")),
        name="gqa",
    )(proj, proj, proj, cos_t, sin_t, qw, kw)


def _xattn_kernel(q_ref, mem_ref, mw_ref, wkv_ref, o_ref, mk_s, mv_s):
    xw = X_HEADS * X_HEAD_DIM

    @pl.when(pl.program_id(1) == 0)
    def _():
        hm = _rms(mem_ref[...], mw_ref[...]).astype(BF16)
        kv = jnp.dot(hm, wkv_ref[...], preferred_element_type=F32)
        mk_s[...] = kv[:, :xw].astype(BF16)
        mv_s[...] = kv[:, xw:].astype(BF16)

    for h in range(X_HEADS):
        cols = slice(h * X_HEAD_DIM, (h + 1) * X_HEAD_DIM)
        s = lax.dot_general(q_ref[:, cols], mk_s[:, cols], NT_DIMS, preferred_element_type=F32)
        s = s * (X_HEAD_DIM ** -0.5)
        p = jnp.exp(s - jnp.max(s, axis=-1, keepdims=True))
        den = jnp.sum(p, axis=-1, keepdims=True)
        o = jnp.dot(p.astype(BF16), mv_s[:, cols], preferred_element_type=F32) / den
        o_ref[:, cols] = o.astype(BF16)


def _cross_attention(proj, mem2, mem_norm_w, w_kv, batch, seq, m_len, q_col0, tm):
    t = batch * seq
    d = mem2.shape[1]
    xw = X_HEADS * X_HEAD_DIM
    q_base = q_col0 * LANES // xw
    n_t = seq // tm
    return pl.pallas_call(
        _xattn_kernel,
        grid=(batch, n_t),
        in_specs=[
            pl.BlockSpec((tm, xw), lambda b, i: (b * n_t + i, q_base)),
            pl.BlockSpec((m_len, d), lambda b, i: (b, 0)),
            pl.BlockSpec((1, d), lambda b, i: (0, 0)),
            pl.BlockSpec((d, 2 * xw), lambda b, i: (0, 0)),
        ],
        out_specs=pl.BlockSpec((tm, xw), lambda b, i: (b * n_t + i, 0)),
        out_shape=jax.ShapeDtypeStruct((t, xw), BF16),
        scratch_shapes=[pltpu.VMEM((m_len, xw), BF16), pltpu.VMEM((m_len, xw), BF16)],
        compiler_params=_params(("arbitrary", "arbitrary")),
        name="xattn",
    )(proj, mem2, mem_norm_w, w_kv)


def _merge_kernel(x_ref, g_ref, ydn_ref, yatt_ref, yx_ref, wb_ref, wo_ref, o_ref):
    d = x_ref.shape[1]
    merged = None
    for n, y_ref in enumerate((ydn_ref, yatt_ref, yx_ref)):
        yb = jnp.dot(y_ref[...], wb_ref[n], preferred_element_type=F32)
        term = _sigmoid(g_ref[:, n * d:(n + 1) * d].astype(F32)) * yb
        merged = term if merged is None else merged + term
    o_ref[...] = x_ref[...] + jnp.dot(merged.astype(BF16), wo_ref[...], preferred_element_type=F32)


def _merge(x2, proj, ydn, yatt, yx, w_branch, w_out, tm):
    t, d = x2.shape
    bw = ydn.shape[1]
    return pl.pallas_call(
        _merge_kernel,
        grid=(t // tm,),
        in_specs=[
            pl.BlockSpec((tm, d), lambda i: (i, 0)),
            pl.BlockSpec((tm, N_BRANCH * d), lambda i: (i, 0)),
            pl.BlockSpec((tm, bw), lambda i: (i, 0)),
            pl.BlockSpec((tm, bw), lambda i: (i, 0)),
            pl.BlockSpec((tm, bw), lambda i: (i, 0)),
            pl.BlockSpec((N_BRANCH, bw, d), lambda i: (0, 0, 0)),
            pl.BlockSpec((d, d), lambda i: (0, 0)),
        ],
        out_specs=pl.BlockSpec((tm, d), lambda i: (i, 0)),
        out_shape=jax.ShapeDtypeStruct((t, d), F32),
        compiler_params=_params(("arbitrary",)),
        name="merge",
    )(x2, proj, ydn, yatt, yx, w_branch, w_out)


def _ffn_kernel(x_ref, nw_ref, wu_ref, wd_ref, fw_ref, o_ref, *, ff_chunk, final_norm):
    x = x_ref[...]
    h = _rms(x, nw_ref[...]).astype(BF16)
    acc = x
    for c0 in range(0, wu_ref.shape[1], ff_chunk):
        u = jnp.dot(h, wu_ref[:, c0:c0 + ff_chunk], preferred_element_type=F32)
        a = jnp.square(jnp.maximum(u, 0.0)).astype(BF16)
        acc = acc + jnp.dot(a, wd_ref[c0:c0 + ff_chunk, :], preferred_element_type=F32)
    o_ref[...] = _rms(acc, fw_ref[...]) if final_norm else acc


def _ffn(x2, norm_w, w_up, w_down, final_w, final_norm, tm):
    t, d = x2.shape
    dff = w_up.shape[1]
    return pl.pallas_call(
        functools.partial(_ffn_kernel, ff_chunk=1024, final_norm=final_norm),
        grid=(t // tm,),
        in_specs=[
            pl.BlockSpec((tm, d), lambda i: (i, 0)),
            pl.BlockSpec((1, d), lambda i: (0, 0)),
            pl.BlockSpec((d, dff), lambda i: (0, 0)),
            pl.BlockSpec((dff, d), lambda i: (0, 0)),
            pl.BlockSpec((1, d), lambda i: (0, 0)),
        ],
        out_specs=pl.BlockSpec((tm, d), lambda i: (i, 0)),
        out_shape=jax.ShapeDtypeStruct((t, d), F32),
        compiler_params=_params(("arbitrary",)),
        name="ffn",
    )(x2, norm_w, w_up, w_down, final_w)


def _gate_lanes(beta_like, alpha_like):
    hp = DN_HEADS_PER_STEP
    lead = beta_like.shape[:-1]
    blocks = []
    for g in range(DN_GROUPS):
        pick = lambda a: [a[..., dd * DN_HEADS + g * hp: dd * DN_HEADS + (g + 1) * hp] for dd in range(2)]
        blocks += pick(beta_like) + [jnp.zeros(lead + (DN_ALPHA_LANE - 2 * hp,), beta_like.dtype)]
        blocks += pick(alpha_like) + [jnp.zeros(lead + (LANES - DN_ALPHA_LANE - 2 * hp,), beta_like.dtype)]
    return jnp.concatenate(blocks, axis=-1)


def _rope_tables(seq):
    pos = jnp.arange(seq)
    inv_freq = ROPE_THETA ** (-jnp.arange(ROPE_PAIRS, dtype=F32) / ROPE_PAIRS)
    ang_r = (pos // GRID_W).astype(F32)[:, None] * inv_freq
    ang_c = (pos % GRID_W).astype(F32)[:, None] * inv_freq
    ang = jnp.concatenate([ang_r, ang_r, ang_c, ang_c] * (LANES // ATT_HEAD_DIM), axis=1)
    return jnp.cos(ang), jnp.sin(ang)


def kernel(x, mem, mix_norm_w, w_in, dn_conv_w, dn_a_log, dn_dt_bias, dn_norm_w, q_norm_w, k_norm_w,
           mem_norm_w, w_mem_kv, w_branch, w_out, ffn_norm_w, w_up, w_down, final_norm_w):
    batch, seq, d = x.shape
    m_len = mem.shape[1]
    depth = w_in.shape[0]
    bw = d // 2
    dn_cols = 3 * bw
    assert seq % DN_BLOCK == 0 and seq % 512 == 0 and DN_HEADS * DN_HEAD_DIM == bw
    assert ATT_HEADS * ATT_HEAD_DIM == bw and X_HEADS * X_HEAD_DIM == bw

    n_gate = N_BRANCH * d
    col_dn = n_gate // LANES
    col_aq = col_dn + (dn_cols + bw) // LANES
    col_xq = col_aq + bw // LANES
    col_ak = col_xq + bw // LANES
    col_av = col_ak + 1
    n_proj = (col_av + 1) * LANES
    o_qkv, o_z, o_b, o_a = 0, dn_cols, dn_cols + bw, dn_cols + bw + 2 * DN_HEADS
    o_aq = o_a + 2 * DN_HEADS
    o_ak = o_aq + bw
    o_av = o_ak + ATT_KV_HEADS * ATT_HEAD_DIM
    o_xq = o_av + ATT_KV_HEADS * ATT_HEAD_DIM
    o_g = o_xq + bw

    cos_t, sin_t = _rope_tables(seq)
    x2 = x.reshape(batch * seq, d)
    mem2 = mem.reshape(batch * m_len, d)
    tile = lambda w: jnp.tile(w, LANES // ATT_HEAD_DIM)[None, :]

    for l in range(depth):
        wl = w_in[l]
        w_prep = jnp.concatenate(
            [wl[:, o_g:], wl[:, o_qkv:o_b], wl[:, o_aq:o_ak], wl[:, o_xq:o_g], wl[:, o_ak:o_xq],
             _gate_lanes(wl[:, o_b:o_a], wl[:, o_a:o_aq])], axis=1).astype(BF16)
        zeros8 = jnp.zeros((2 * DN_HEADS,), F32)
        gpar = jnp.concatenate(
            [_gate_lanes(zeros8, dn_a_log[l].reshape(-1)).reshape(DN_GROUPS, 1, LANES),
             _gate_lanes(zeros8, dn_dt_bias[l].reshape(-1)).reshape(DN_GROUPS, 1, LANES),
             jnp.zeros((DN_GROUPS, 6, LANES), F32)], axis=1)

        proj, ba = _inproj(x2, mix_norm_w[l][None, :], w_prep, n_proj, tm=512)
        ydn = _deltanet(proj, ba, dn_conv_w[l], gpar, dn_norm_w[l][None, :], batch, seq, col_dn)
        yatt = _attention(proj, cos_t, sin_t, tile(q_norm_w[l]), tile(k_norm_w[l]), batch, seq,
                          col_aq, col_ak, col_av, tq=256)
        yx = _cross_attention(proj, mem2, mem_norm_w[l][None, :], w_mem_kv[l].astype(BF16),
                              batch, seq, m_len, col_xq, tm=512)
        x2 = _merge(x2, proj, ydn, yatt, yx, w_branch[l].astype(BF16), w_out[l].astype(BF16), tm=512)
        x2 = _ffn(x2, ffn_norm_w[l][None, :], w_up[l].astype(BF16), w_down[l].astype(BF16),
                  final_norm_w[None, :], final_norm=(l == depth - 1), tm=512)
    return x2.reshape(batch, seq, d)
```

```python
import functools

import jax
import jax.numpy as jnp
from jax import lax
from jax.experimental import pallas as pl
from jax.experimental.pallas import tpu as pltpu

F32 = jnp.float32
BF16 = jnp.bfloat16

DN_HEADS = 4
DN_HEAD_DIM = 128
CONV_W = 5
ATT_HEADS = 8
ATT_KV_HEADS = 2
ATT_HEAD_DIM = 64
ROPE_PAIRS = ATT_HEAD_DIM // 4
ROPE_THETA = 10000.0
GRID_W = 64
X_HEADS = 4
X_HEAD_DIM = 128
N_BRANCH = 3
EPS = 1e-6

LANES = 128
V7X_VMEM_LIMIT = 56 * 1024 * 1024

DN_CHUNK = 64
DN_BLOCK = 256
DN_HEADS_PER_STEP = 2
DN_GROUPS = DN_HEADS // DN_HEADS_PER_STEP
DN_ALPHA_LANE = 8

NT_DIMS = (((1,), (1,)), ((), ()))
TN_DIMS = (((0,), (0,)), ((), ()))
NEG_BIG = -1e30


def _rms(x, w):
    return (x * lax.rsqrt(jnp.mean(x * x, axis=-1, keepdims=True) + EPS)) * w


def _sigmoid(x):
    return 1.0 / (1.0 + jnp.exp(-x))


def _softplus(x):
    return jnp.maximum(x, 0.0) + jnp.log(1.0 + jnp.exp(-jnp.abs(x)))


def _params(semantics):
    return pltpu.CompilerParams(dimension_semantics=semantics, vmem_limit_bytes=V7X_VMEM_LIMIT)


def _inproj_kernel(x_ref, nw_ref, w_ref, proj_ref, ba_ref, *, n_proj, col_chunk):
    h = _rms(x_ref[...], nw_ref[...]).astype(BF16)
    for c0 in range(0, n_proj, col_chunk):
        c1 = min(c0 + col_chunk, n_proj)
        proj_ref[:, c0:c1] = jnp.dot(h, w_ref[:, c0:c1], preferred_element_type=F32).astype(BF16)
    ba_ref[...] = jnp.dot(h, w_ref[:, n_proj:], preferred_element_type=F32)


def _inproj(x2, norm_w, w_prep, n_proj, tm):
    t, d = x2.shape
    n_all = w_prep.shape[1]
    return pl.pallas_call(
        functools.partial(_inproj_kernel, n_proj=n_proj, col_chunk=512),
        grid=(t // tm,),
        in_specs=[
            pl.BlockSpec((tm, d), lambda i: (i, 0)),
            pl.BlockSpec((1, d), lambda i: (0, 0)),
            pl.BlockSpec((d, n_all), lambda i: (0, 0)),
        ],
        out_specs=[
            pl.BlockSpec((tm, n_proj), lambda i: (i, 0)),
            pl.BlockSpec((tm, n_all - n_proj), lambda i: (i, 0)),
        ],
        out_shape=[
            jax.ShapeDtypeStruct((t, n_proj), BF16),
            jax.ShapeDtypeStruct((t, n_all - n_proj), F32),
        ],
        compiler_params=_params(("arbitrary",)),
        name="inproj",
    )(x2, norm_w, w_prep)


def _deltanet_kernel(dq_ref, dk_ref, dv_ref, z_ref, ba_ref, wq_ref, wk_ref, wv_ref, gpar_ref, nw_ref,
                     y_ref, col_s, row_s, q_s, k_s, v_s, o_s, st_s, mask_s, *, seq, hp):
    c = DN_CHUNK
    blk = DN_BLOCK
    n_blk = seq // blk
    per_blk = blk // c
    hd = DN_HEAD_DIM

    ba = ba_ref[...]
    beta = _sigmoid(ba)
    g = -jnp.exp(gpar_ref[0:1, :]) * _softplus(ba + gpar_ref[1:2, :])
    rin = lax.broadcasted_iota(jnp.int32, (seq, LANES), 0) & (c - 1)
    pre = g
    suf = g
    sh = 1
    while sh < c:
        pre = pre + jnp.where(rin >= sh, pltpu.roll(pre, sh, 0), 0.0)
        suf = suf + jnp.where(rin < c - sh, pltpu.roll(suf, seq - sh, 0), 0.0)
        sh *= 2
    col_s[0] = beta
    col_s[1] = pre
    col_s[2] = suf
    col_s[3] = pre + suf - g
    row_s[0] = pre.T
    row_s[1] = suf.T

    ii = lax.broadcasted_iota(jnp.int32, (blk, blk), 0)
    jj = lax.broadcasted_iota(jnp.int32, (blk, blk), 1)
    same = (ii // c) == (jj // c)
    mask_s[0] = jnp.where(jnp.logical_and(same, ii >= jj), 0.0, NEG_BIG)
    mask_s[1] = jnp.where(jnp.logical_and(same, ii <= jj), 0.0, NEG_BIG)
    mask_s[2] = jnp.where(ii != jj, 1.0, 0.0)

    tpos = lax.broadcasted_iota(jnp.int32, (seq, hd), 0)

    def conv_silu(x_ref, w_ref, hh):
        x = x_ref[:, hh * hd:(hh + 1) * hd].astype(F32)
        w = w_ref[:, hh * hd:(hh + 1) * hd]
        y = x * w[2:3, :]
        y = y + jnp.where(tpos >= 2, pltpu.roll(x, 2, 0), 0.0) * w[0:1, :]
        y = y + jnp.where(tpos >= 1, pltpu.roll(x, 1, 0), 0.0) * w[1:2, :]
        y = y + jnp.where(tpos < seq - 1, pltpu.roll(x, seq - 1, 0), 0.0) * w[3:4, :]
        y = y + jnp.where(tpos < seq - 2, pltpu.roll(x, seq - 2, 0), 0.0) * w[4:5, :]
        return y * _sigmoid(y)

    def l2n(x):
        return x * lax.rsqrt(jnp.sum(x * x, axis=-1, keepdims=True) + EPS)

    for hh in range(hp):
        q_s[hh] = l2n(conv_silu(dq_ref, wq_ref, hh)) * (hd ** -0.5)
        k_s[hh] = l2n(conv_silu(dk_ref, wk_ref, hh))
        v_s[hh] = conv_silu(dv_ref, wv_ref, hh)
    o_s[...] = jnp.zeros(o_s.shape, F32)
    st_s[...] = jnp.zeros(st_s.shape, F32)

    n_double = c.bit_length() - 2

    def block_step(r, carry):
        ld = []
        for hh in range(hp):
            for d in range(2):
                b_i = r if d == 0 else n_blk - 1 - r
                r0 = pl.multiple_of(b_i * blk, blk)
                lane_b = d * hp + hh
                lane_g = DN_ALPHA_LANE + d * hp + hh
                ld.append(dict(
                    r0=r0, d=d, hh=hh,
                    q=q_s[hh, pl.ds(r0, blk), :], k=k_s[hh, pl.ds(r0, blk), :], v=v_s[hh, pl.ds(r0, blk), :],
                    bcol=col_s[0, pl.ds(r0, blk), :][:, lane_b:lane_b + 1],
                    gcol=col_s[1 + d, pl.ds(r0, blk), :][:, lane_g:lane_g + 1],
                    tcol=col_s[3, pl.ds(r0, blk), :][:, lane_g:lane_g + 1],
                    grow=row_s[d, lane_g:lane_g + 1, pl.ds(r0, blk)]))
        for e in ld:
            e["dec"] = jnp.exp((e["gcol"] - e["grow"]) + mask_s[e["d"]])
            e["kb"] = e["k"] * e["bcol"]
            e["k16"] = e["k"].astype(BF16)
        for e in ld:
            e["kk"] = lax.dot_general(e["kb"].astype(BF16), e["k16"], NT_DIMS, preferred_element_type=F32)
        for e in ld:
            e["qk"] = lax.dot_general(e["q"].astype(BF16), e["k16"], NT_DIMS, preferred_element_type=F32)
        for e in ld:
            lm = e["kk"] * e["dec"] * mask_s[2]
            e["attn"] = (e["qk"] * e["dec"]).astype(BF16)
            e["x"] = -lm
            e["l16"] = lm.astype(BF16)
        for e in ld:
            e["m"] = jnp.dot(e["l16"], e["l16"], preferred_element_type=F32)
        for it in range(n_double):
            last = it == n_double - 1
            for e in ld:
                m16 = e["m"].astype(BF16)
                x16 = e["x"].astype(BF16)
                lhs = x16 if last else jnp.concatenate([x16, m16], axis=0)
                e["both"] = jnp.dot(lhs, m16, preferred_element_type=F32)
            for e in ld:
                e["x"] = e["x"] + e["m"] + e["both"][:blk]
                if not last:
                    e["m"] = e["both"][blk:]
        for e in ld:
            e["eg"] = jnp.exp(e["gcol"])
            e["rhs"] = jnp.concatenate([e["v"] * e["bcol"], e["kb"] * e["eg"]], axis=1)
        for e in ld:
            e["uw"] = e["rhs"] + jnp.dot(e["x"].astype(BF16), e["rhs"].astype(BF16), preferred_element_type=F32)
        for i, e in enumerate(ld):
            e["u"] = e["uw"][:, :hd]
            e["w16"] = e["uw"][:, hd:].astype(BF16)
            e["qd16"] = (e["q"] * e["eg"]).astype(BF16)
            e["kd16"] = (e["k"] * jnp.exp(e["tcol"] - e["gcol"])).astype(BF16)
            e["state"] = st_s[i]
        for step in range(per_blk):
            for e in ld:
                ci = step if e["d"] == 0 else per_blk - 1 - step
                lo, hi = ci * c, (ci + 1) * c
                e["lo"], e["hi"] = lo, hi
                lhs = jnp.concatenate([e["w16"][lo:hi], e["qd16"][lo:hi]], axis=0)
                e["ws"] = jnp.dot(lhs, e["state"].astype(BF16), preferred_element_type=F32)
            for e in ld:
                lo, hi = e["lo"], e["hi"]
                e["vn16"] = (e["u"][lo:hi] - e["ws"][:c]).astype(BF16)
                e["av"] = jnp.dot(e["attn"][lo:hi, lo:hi], e["vn16"], preferred_element_type=F32)
            for e in ld:
                lo, hi = e["lo"], e["hi"]
                e["kv"] = lax.dot_general(e["kd16"][lo:hi], e["vn16"], TN_DIMS, preferred_element_type=F32)
            for e in ld:
                lo = e["lo"]
                e["state"] = e["state"] * jnp.exp(e["tcol"][lo:lo + 1, :]) + e["kv"]
                rows = pl.ds(e["r0"] + lo, c)
                o_s[e["hh"], rows, :] = o_s[e["hh"], rows, :] + (e["ws"][c:] + e["av"])
        for i, e in enumerate(ld):
            st_s[i] = e["state"]
        return carry

    lax.fori_loop(0, n_blk, block_step, 0)

    for hh in range(hp):
        zz = z_ref[:, hh * hd:(hh + 1) * hd].astype(F32)
        y_ref[:, hh * hd:(hh + 1) * hd] = (_rms(o_s[hh], nw_ref[...]) * (zz * _sigmoid(zz))).astype(BF16)


def _deltanet(proj, ba, conv_w, gpar, norm_w, batch, seq, col0):
    hp = DN_HEADS_PER_STEP
    wblk = hp * DN_HEAD_DIM
    base = col0 * LANES // wblk
    per = DN_HEADS // hp
    t = batch * seq
    n_chain = 2 * hp
    return pl.pallas_call(
        functools.partial(_deltanet_kernel, seq=seq, hp=hp),
        grid=(batch, per),
        in_specs=[
            pl.BlockSpec((seq, wblk), lambda b, g: (b, base + g)),
            pl.BlockSpec((seq, wblk), lambda b, g: (b, base + per + g)),
            pl.BlockSpec((seq, wblk), lambda b, g: (b, base + 2 * per + g)),
            pl.BlockSpec((seq, wblk), lambda b, g: (b, base + 3 * per + g)),
            pl.BlockSpec((seq, LANES), lambda b, g: (b, g)),
            pl.BlockSpec((CONV_W, wblk), lambda b, g: (0, g)),
            pl.BlockSpec((CONV_W, wblk), lambda b, g: (0, per + g)),
            pl.BlockSpec((CONV_W, wblk), lambda b, g: (0, 2 * per + g)),
            pl.BlockSpec((None, 8, LANES), lambda b, g: (g, 0, 0)),
            pl.BlockSpec((1, DN_HEAD_DIM), lambda b, g: (0, 0)),
        ],
        out_specs=pl.BlockSpec((seq, wblk), lambda b, g: (b, g)),
        out_shape=jax.ShapeDtypeStruct((t, DN_HEADS * DN_HEAD_DIM), BF16),
        scratch_shapes=[
            pltpu.VMEM((4, seq, LANES), F32),
            pltpu.VMEM((2, LANES, seq), F32),
            pltpu.VMEM((hp, seq, DN_HEAD_DIM), F32),
            pltpu.VMEM((hp, seq, DN_HEAD_DIM), F32),
            pltpu.VMEM((hp, seq, DN_HEAD_DIM), F32),
            pltpu.VMEM((hp, seq, DN_HEAD_DIM), F32),
            pltpu.VMEM((n_chain, DN_HEAD_DIM, DN_HEAD_DIM), F32),
            pltpu.VMEM((3, DN_BLOCK, DN_BLOCK), F32),
        ],
        compiler_params=_params(("arbitrary", "arbitrary")),
        name="deltanet",
    )(proj, proj, proj, proj, ba, conv_w, conv_w, conv_w, gpar, norm_w)


def _attn_kernel(q_ref, k_ref, v_ref, cos_ref, sin_ref, qw_ref, kw_ref, o_ref,
                 klo_s, khi_s, vlo_s, vhi_s, *, tq):
    g = pl.program_id(1)
    qi = pl.program_id(2)
    lane = lax.broadcasted_iota(jnp.int32, (1, LANES), 1)
    low = lane < ATT_HEAD_DIM
    first = (lane & (2 * ROPE_PAIRS - 1)) < ROPE_PAIRS

    def norm_rope(x, w, cos, sin):
        sq = x * x
        ms_lo = jnp.sum(jnp.where(low, sq, 0.0), axis=-1, keepdims=True)
        ms_hi = jnp.sum(jnp.where(low, 0.0, sq), axis=-1, keepdims=True)
        ms = jnp.where(low, ms_lo, ms_hi) * (1.0 / ATT_HEAD_DIM)
        y = (x * lax.rsqrt(ms + EPS)) * w
        rot = jnp.where(first, -pltpu.roll(y, LANES - ROPE_PAIRS, 1), pltpu.roll(y, ROPE_PAIRS, 1))
        return y * cos + rot * sin

    @pl.when(qi == 0)
    def _():
        in_g = (lane >= ATT_HEAD_DIM).astype(jnp.int32) == g
        is0 = g == 0
        kr = norm_rope(k_ref[...].astype(F32), kw_ref[...], cos_ref[...], sin_ref[...])
        for src, lo_s, hi_s in ((kr, klo_s, khi_s), (v_ref[...].astype(F32), vlo_s, vhi_s)):
            sel = jnp.where(in_g, src, 0.0)
            oth = pltpu.roll(sel, ATT_HEAD_DIM, 1)
            lo_s[...] = jnp.where(is0, sel, oth).astype(BF16)
            hi_s[...] = jnp.where(is0, oth, sel).astype(BF16)

    r0 = pl.multiple_of(qi * tq, tq)
    cos_q = cos_ref[pl.ds(r0, tq), :]
    sin_q = sin_ref[pl.ds(r0, tq), :]
    for jb in range(q_ref.shape[1] // LANES):
        qb = norm_rope(q_ref[:, jb * LANES:(jb + 1) * LANES].astype(F32), qw_ref[...], cos_q, sin_q)
        qb = (qb * (ATT_HEAD_DIM ** -0.5)).astype(BF16)
        acc = None
        for k_s, v_s in ((klo_s, vlo_s), (khi_s, vhi_s)):
            s = lax.dot_general(qb, k_s[...], NT_DIMS, preferred_element_type=F32)
            p = jnp.exp(s - jnp.max(s, axis=-1, keepdims=True))
            den = jnp.sum(p, axis=-1, keepdims=True)
            o = jnp.dot(p.astype(BF16), v_s[...], preferred_element_type=F32) / den
            acc = o if acc is None else acc + o
        o_ref[:, jb * LANES:(jb + 1) * LANES] = acc.astype(BF16)


def _attention(proj, cos_t, sin_t, qw, kw, batch, seq, q_col0, k_col, v_col, tq):
    t = batch * seq
    q_w = ATT_HEADS * ATT_HEAD_DIM // ATT_KV_HEADS
    q_base = q_col0 * LANES // q_w
    n_q = seq // tq
    return pl.pallas_call(
        functools.partial(_attn_kernel, tq=tq),
        grid=(batch, ATT_KV_HEADS, n_q),
        in_specs=[
            pl.BlockSpec((tq, q_w), lambda b, g, i: (b * n_q + i, q_base + g)),
            pl.BlockSpec((seq, LANES), lambda b, g, i: (b, k_col)),
            pl.BlockSpec((seq, LANES), lambda b, g, i: (b, v_col)),
            pl.BlockSpec((seq, LANES), lambda b, g, i: (0, 0)),
            pl.BlockSpec((seq, LANES), lambda b, g, i: (0, 0)),
            pl.BlockSpec((1, LANES), lambda b, g, i: (0, 0)),
            pl.BlockSpec((1, LANES), lambda b, g, i: (0, 0)),
        ],
        out_specs=pl.BlockSpec((tq, q_w), lambda b, g, i: (b * n_q + i, g)),
        out_shape=jax.ShapeDtypeStruct((t, ATT_HEADS * ATT_HEAD_DIM), BF16),
        scratch_shapes=[pltpu.VMEM((seq, LANES), BF16) for _ in range(4)],
        compiler_params=_params(("arbitrary", "arbitrary", "arbitrary")),
        name="gqa",
    )(proj, proj, proj, cos_t, sin_t, qw, kw)


def _xattn_kernel(q_ref, mem_ref, mw_ref, wkv_ref, o_ref, mk_s, mv_s):
    xw = X_HEADS * X_HEAD_DIM

    @pl.when(pl.program_id(1) == 0)
    def _():
        hm = _rms(mem_ref[...], mw_ref[...]).astype(BF16)
        kv = jnp.dot(hm, wkv_ref[...], preferred_element_type=F32)
        mk_s[...] = kv[:, :xw].astype(BF16)
        mv_s[...] = kv[:, xw:].astype(BF16)

    for h in range(X_HEADS):
        cols = slice(h * X_HEAD_DIM, (h + 1) * X_HEAD_DIM)
        s = lax.dot_general(q_ref[:, cols], mk_s[:, cols], NT_DIMS, preferred_element_type=F32)
        s = s * (X_HEAD_DIM ** -0.5)
        p = jnp.exp(s - jnp.max(s, axis=-1, keepdims=True))
        den = jnp.sum(p, axis=-1, keepdims=True)
        o = jnp.dot(p.astype(BF16), mv_s[:, cols], preferred_element_type=F32) / den
        o_ref[:, cols] = o.astype(BF16)


def _cross_attention(proj, mem2, mem_norm_w, w_kv, batch, seq, m_len, q_col0, tm):
    t = batch * seq
    d = mem2.shape[1]
    xw = X_HEADS * X_HEAD_DIM
    q_base = q_col0 * LANES // xw
    n_t = seq // tm
    return pl.pallas_call(
        _xattn_kernel,
        grid=(batch, n_t),
        in_specs=[
            pl.BlockSpec((tm, xw), lambda b, i: (b * n_t + i, q_base)),
            pl.BlockSpec((m_len, d), lambda b, i: (b, 0)),
            pl.BlockSpec((1, d), lambda b, i: (0, 0)),
            pl.BlockSpec((d, 2 * xw), lambda b, i: (0, 0)),
        ],
        out_specs=pl.BlockSpec((tm, xw), lambda b, i: (b * n_t + i, 0)),
        out_shape=jax.ShapeDtypeStruct((t, xw), BF16),
        scratch_shapes=[pltpu.VMEM((m_len, xw), BF16), pltpu.VMEM((m_len, xw), BF16)],
        compiler_params=_params(("arbitrary", "arbitrary")),
        name="xattn",
    )(proj, mem2, mem_norm_w, w_kv)


def _merge_kernel(x_ref, g_ref, ydn_ref, yatt_ref, yx_ref, wb_ref, wo_ref, o_ref):
    d = x_ref.shape[1]
    merged = None
    for n, y_ref in enumerate((ydn_ref, yatt_ref, yx_ref)):
        yb = jnp.dot(y_ref[...], wb_ref[n], preferred_element_type=F32)
        term = _sigmoid(g_ref[:, n * d:(n + 1) * d].astype(F32)) * yb
        merged = term if merged is None else merged + term
    o_ref[...] = x_ref[...] + jnp.dot(merged.astype(BF16), wo_ref[...], preferred_element_type=F32)


def _merge(x2, proj, ydn, yatt, yx, w_branch, w_out, tm):
    t, d = x2.shape
    bw = ydn.shape[1]
    return pl.pallas_call(
        _merge_kernel,
        grid=(t // tm,),
        in_specs=[
            pl.BlockSpec((tm, d), lambda i: (i, 0)),
            pl.BlockSpec((tm, N_BRANCH * d), lambda i: (i, 0)),
            pl.BlockSpec((tm, bw), lambda i: (i, 0)),
            pl.BlockSpec((tm, bw), lambda i: (i, 0)),
            pl.BlockSpec((tm, bw), lambda i: (i, 0)),
            pl.BlockSpec((N_BRANCH, bw, d), lambda i: (0, 0, 0)),
            pl.BlockSpec((d, d), lambda i: (0, 0)),
        ],
        out_specs=pl.BlockSpec((tm, d), lambda i: (i, 0)),
        out_shape=jax.ShapeDtypeStruct((t, d), F32),
        compiler_params=_params(("arbitrary",)),
        name="merge",
    )(x2, proj, ydn, yatt, yx, w_branch, w_out)


def _ffn_kernel(x_ref, nw_ref, wu_ref, wd_ref, fw_ref, o_ref, *, ff_chunk, final_norm):
    x = x_ref[...]
    h = _rms(x, nw_ref[...]).astype(BF16)
    acc = x
    for c0 in range(0, wu_ref.shape[1], ff_chunk):
        u = jnp.dot(h, wu_ref[:, c0:c0 + ff_chunk], preferred_element_type=F32)
        a = jnp.square(jnp.maximum(u, 0.0)).astype(BF16)
        acc = acc + jnp.dot(a, wd_ref[c0:c0 + ff_chunk, :], preferred_element_type=F32)
    o_ref[...] = _rms(acc, fw_ref[...]) if final_norm else acc


def _ffn(x2, norm_w, w_up, w_down, final_w, final_norm, tm):
    t, d = x2.shape
    dff = w_up.shape[1]
    return pl.pallas_call(
        functools.partial(_ffn_kernel, ff_chunk=1024, final_norm=final_norm),
        grid=(t // tm,),
        in_specs=[
            pl.BlockSpec((tm, d), lambda i: (i, 0)),
            pl.BlockSpec((1, d), lambda i: (0, 0)),
            pl.BlockSpec((d, dff), lambda i: (0, 0)),
            pl.BlockSpec((dff, d), lambda i: (0, 0)),
            pl.BlockSpec((1, d), lambda i: (0, 0)),
        ],
        out_specs=pl.BlockSpec((tm, d), lambda i: (i, 0)),
        out_shape=jax.ShapeDtypeStruct((t, d), F32),
        compiler_params=_params(("arbitrary",)),
        name="ffn",
    )(x2, norm_w, w_up, w_down, final_w)


def _gate_lanes(beta_like, alpha_like):
    hp = DN_HEADS_PER_STEP
    lead = beta_like.shape[:-1]
    blocks = []
    for g in range(DN_GROUPS):
        pick = lambda a: [a[..., dd * DN_HEADS + g * hp: dd * DN_HEADS + (g + 1) * hp] for dd in range(2)]
        blocks += pick(beta_like) + [jnp.zeros(lead + (DN_ALPHA_LANE - 2 * hp,), beta_like.dtype)]
        blocks += pick(alpha_like) + [jnp.zeros(lead + (LANES - DN_ALPHA_LANE - 2 * hp,), beta_like.dtype)]
    return jnp.concatenate(blocks, axis=-1)


def _rope_tables(seq):
    pos = jnp.arange(seq)
    inv_freq = ROPE_THETA ** (-jnp.arange(ROPE_PAIRS, dtype=F32) / ROPE_PAIRS)
    ang_r = (pos // GRID_W).astype(F32)[:, None] * inv_freq
    ang_c = (pos % GRID_W).astype(F32)[:, None] * inv_freq
    ang = jnp.concatenate([ang_r, ang_r, ang_c, ang_c] * (LANES // ATT_HEAD_DIM), axis=1)
    return jnp.cos(ang), jnp.sin(ang)


def kernel(x, mem, mix_norm_w, w_in, dn_conv_w, dn_a_log, dn_dt_bias, dn_norm_w, q_norm_w, k_norm_w,
           mem_norm_w, w_mem_kv, w_branch, w_out, ffn_norm_w, w_up, w_down, final_norm_w):
    batch, seq, d = x.shape
    m_len = mem.shape[1]
    depth = w_in.shape[0]
    bw = d // 2
    dn_cols = 3 * bw
    assert seq % DN_BLOCK == 0 and seq % 512 == 0 and DN_HEADS * DN_HEAD_DIM == bw
    assert ATT_HEADS * ATT_HEAD_DIM == bw and X_HEADS * X_HEAD_DIM == bw

    n_gate = N_BRANCH * d
    col_dn = n_gate // LANES
    col_aq = col_dn + (dn_cols + bw) // LANES
    col_xq = col_aq + bw // LANES
    col_ak = col_xq + bw // LANES
    col_av = col_ak + 1
    n_proj = (col_av + 1) * LANES
    o_qkv, o_z, o_b, o_a = 0, dn_cols, dn_cols + bw, dn_cols + bw + 2 * DN_HEADS
    o_aq = o_a + 2 * DN_HEADS
    o_ak = o_aq + bw
    o_av = o_ak + ATT_KV_HEADS * ATT_HEAD_DIM
    o_xq = o_av + ATT_KV_HEADS * ATT_HEAD_DIM
    o_g = o_xq + bw

    cos_t, sin_t = _rope_tables(seq)
    x2 = x.reshape(batch * seq, d)
    mem2 = mem.reshape(batch * m_len, d)
    tile = lambda w: jnp.tile(w, LANES // ATT_HEAD_DIM)[None, :]

    for l in range(depth):
        wl = w_in[l]
        w_prep = jnp.concatenate(
            [wl[:, o_g:], wl[:, o_qkv:o_b], wl[:, o_aq:o_ak], wl[:, o_xq:o_g], wl[:, o_ak:o_xq],
             _gate_lanes(wl[:, o_b:o_a], wl[:, o_a:o_aq])], axis=1).astype(BF16)
        zeros8 = jnp.zeros((2 * DN_HEADS,), F32)
        gpar = jnp.concatenate(
            [_gate_lanes(zeros8, dn_a_log[l].reshape(-1)).reshape(DN_GROUPS, 1, LANES),
             _gate_lanes(zeros8, dn_dt_bias[l].reshape(-1)).reshape(DN_GROUPS, 1, LANES),
             jnp.zeros((DN_GROUPS, 6, LANES), F32)], axis=1)

        proj, ba = _inproj(x2, mix_norm_w[l][None, :], w_prep, n_proj, tm=512)
        ydn = _deltanet(proj, ba, dn_conv_w[l], gpar, dn_norm_w[l][None, :], batch, seq, col_dn)
        yatt = _attention(proj, cos_t, sin_t, tile(q_norm_w[l]), tile(k_norm_w[l]), batch, seq,
                          col_aq, col_ak, col_av, tq=256)
        yx = _cross_attention(proj, mem2, mem_norm_w[l][None, :], w_mem_kv[l].astype(BF16),
                              batch, seq, m_len, col_xq, tm=512)
        x2 = _merge(x2, proj, ydn, yatt, yx, w_branch[l].astype(BF16), w_out[l].astype(BF16), tm=512)
        x2 = _ffn(x2, ffn_norm_w[l][None, :], w_up[l].astype(BF16), w_down[l].astype(BF16),
                  final_norm_w[None, :], final_norm=(l == depth - 1), tm=512)
    return x2.reshape(batch, seq, d)
```

```python
import functools

import jax
import jax.numpy as jnp
from jax import lax
from jax.experimental import pallas as pl
from jax.experimental.pallas import tpu as pltpu

F32 = jnp.float32
BF16 = jnp.bfloat16

DN_HEADS = 4
DN_HEAD_DIM = 128
CONV_W = 5
ATT_HEADS = 8
ATT_KV_HEADS = 2
ATT_HEAD_DIM = 64
ROPE_PAIRS = ATT_HEAD_DIM // 4
ROPE_THETA = 10000.0
GRID_W = 64
X_HEADS = 4
X_HEAD_DIM = 128
N_BRANCH = 3
EPS = 1e-6
LOG2_E = 1.4426950408889634

LANES = 128
V7X_VMEM_LIMIT = 56 * 1024 * 1024

DN_CHUNK = 64
DN_BLOCK = 256
DN_HEADS_PER_STEP = 2
DN_GROUPS = DN_HEADS // DN_HEADS_PER_STEP
DN_ALPHA_LANE = 8
DN_GATE_ROWS = 2 * DN_ALPHA_LANE
DN_COL_BETA, DN_COL_GC, DN_COL_TOT, DN_COL_EG, DN_COL_EKD, DN_COL_ETOT = 0, 1, 3, 4, 6, 8

NT_DIMS = (((1,), (1,)), ((), ()))
TN_DIMS = (((0,), (0,)), ((), ()))
NEG_BIG = -1e30


def _rms(x, w):
    return (x * lax.rsqrt(jnp.mean(x * x, axis=-1, keepdims=True) + EPS)) * w


def _sigmoid(x):
    return 1.0 / (1.0 + jnp.exp(-x))


def _softplus(x):
    return jnp.maximum(x, 0.0) + jnp.log(1.0 + jnp.exp(-jnp.abs(x)))


def _params(semantics):
    return pltpu.CompilerParams(dimension_semantics=semantics, vmem_limit_bytes=V7X_VMEM_LIMIT)


def _inproj_kernel(x_ref, nw_ref, w_ref, proj_ref, ba_ref, *, n_proj, col_chunk):
    h = _rms(x_ref[...], nw_ref[...]).astype(BF16)
    for c0 in range(0, n_proj, col_chunk):
        c1 = min(c0 + col_chunk, n_proj)
        proj_ref[:, c0:c1] = jnp.dot(h, w_ref[:, c0:c1], preferred_element_type=F32).astype(BF16)
    ba_ref[...] = jnp.dot(h, w_ref[:, n_proj:], preferred_element_type=F32)


def _inproj(x2, norm_w, w_prep, n_proj, tm):
    t, d = x2.shape
    n_all = w_prep.shape[1]
    return pl.pallas_call(
        functools.partial(_inproj_kernel, n_proj=n_proj, col_chunk=512),
        grid=(t // tm,),
        in_specs=[
            pl.BlockSpec((tm, d), lambda i: (i, 0)),
            pl.BlockSpec((1, d), lambda i: (0, 0)),
            pl.BlockSpec((d, n_all), lambda i: (0, 0)),
        ],
        out_specs=[
            pl.BlockSpec((tm, n_proj), lambda i: (i, 0)),
            pl.BlockSpec((tm, n_all - n_proj), lambda i: (i, 0)),
        ],
        out_shape=[
            jax.ShapeDtypeStruct((t, n_proj), BF16),
            jax.ShapeDtypeStruct((t, n_all - n_proj), F32),
        ],
        compiler_params=_params(("arbitrary",)),
        name="inproj",
    )(x2, norm_w, w_prep)


def _deltanet_kernel(dq_ref, dk_ref, dv_ref, z_ref, ba_ref, wq_ref, wk_ref, wv_ref, gpar_ref, nw_ref,
                     y_ref, col_s, row_s, q_s, k_s, v_s, o_s, st_s, mask_s, *, seq, hp):
    c = DN_CHUNK
    blk = DN_BLOCK
    n_blk = seq // blk
    per_blk = blk // c
    hd = DN_HEAD_DIM

    gates = ba_ref[...].T[0:DN_GATE_ROWS, :]
    beta = _sigmoid(gates)
    g = -jnp.exp(gpar_ref[0][:, 0:1]) * _softplus(gates + gpar_ref[1][:, 0:1])
    lin = lax.broadcasted_iota(jnp.int32, (DN_GATE_ROWS, seq), 1) & (c - 1)
    pre = g
    suf = g
    sh = 1
    while sh < c:
        pre = pre + jnp.where(lin >= sh, pltpu.roll(pre, sh, 1), 0.0)
        suf = suf + jnp.where(lin < c - sh, pltpu.roll(suf, seq - sh, 1), 0.0)
        sh *= 2
    tot = pre + suf - g
    row_s[0] = pre
    row_s[1] = suf
    a0 = DN_ALPHA_LANE
    packed = [beta[0:a0], pre[a0:], suf[a0:], tot[a0:], jnp.exp(pre[a0:]), jnp.exp(suf[a0:]),
              jnp.exp(tot[a0:] - pre[a0:]), jnp.exp(tot[a0:] - suf[a0:]), jnp.exp(tot[a0:])]
    packed.append(jnp.zeros((LANES - a0 * len(packed), seq), F32))
    col_s[...] = jnp.concatenate(packed, axis=0).T

    ii = lax.broadcasted_iota(jnp.int32, (blk, blk), 0)
    jj = lax.broadcasted_iota(jnp.int32, (blk, blk), 1)
    same = (ii // c) == (jj // c)
    mask_s[0] = jnp.where(jnp.logical_and(same, ii >= jj), 0.0, NEG_BIG)
    mask_s[1] = jnp.where(jnp.logical_and(same, ii <= jj), 0.0, NEG_BIG)
    mask_s[2] = jnp.where(ii != jj, 1.0, 0.0)

    tpos = lax.broadcasted_iota(jnp.int32, (seq, hd), 0)

    def conv_silu(x_ref, w_ref, hh):
        x = x_ref[:, hh * hd:(hh + 1) * hd].astype(F32)
        w = w_ref[:, hh * hd:(hh + 1) * hd]
        y = x * w[2:3, :]
        y = y + jnp.where(tpos >= 2, pltpu.roll(x, 2, 0), 0.0) * w[0:1, :]
        y = y + jnp.where(tpos >= 1, pltpu.roll(x, 1, 0), 0.0) * w[1:2, :]
        y = y + jnp.where(tpos < seq - 1, pltpu.roll(x, seq - 1, 0), 0.0) * w[3:4, :]
        y = y + jnp.where(tpos < seq - 2, pltpu.roll(x, seq - 2, 0), 0.0) * w[4:5, :]
        return y * _sigmoid(y)

    def l2n(x):
        return x * lax.rsqrt(jnp.sum(x * x, axis=-1, keepdims=True) + EPS)

    for hh in range(hp):
        q_s[hh] = l2n(conv_silu(dq_ref, wq_ref, hh)) * (hd ** -0.5)
        k_s[hh] = l2n(conv_silu(dk_ref, wk_ref, hh))
        v_s[hh] = conv_silu(dv_ref, wv_ref, hh)
    o_s[...] = jnp.zeros(o_s.shape, F32)
    st_s[...] = jnp.zeros(st_s.shape, F32)

    n_double = c.bit_length() - 2

    def block_step(r, carry):
        ld = []
        for hh in range(hp):
            for d in range(2):
                b_i = r if d == 0 else n_blk - 1 - r
                r0 = pl.multiple_of(b_i * blk, blk)
                ci = d * hp + hh
                cols = col_s[pl.ds(r0, blk), :]
                col = lambda group: cols[:, DN_ALPHA_LANE * group + ci:DN_ALPHA_LANE * group + ci + 1]
                ld.append(dict(
                    r0=r0, d=d, hh=hh,
                    q=q_s[hh, pl.ds(r0, blk), :], k=k_s[hh, pl.ds(r0, blk), :], v=v_s[hh, pl.ds(r0, blk), :],
                    bcol=col(DN_COL_BETA), gcol=col(DN_COL_GC + d), eg=col(DN_COL_EG + d),
                    ekd=col(DN_COL_EKD + d), etot=col(DN_COL_ETOT),
                    grow=row_s[d, DN_ALPHA_LANE + ci:DN_ALPHA_LANE + ci + 1, pl.ds(r0, blk)]))
        for e in ld:
            e["dec"] = jnp.exp((e["gcol"] - e["grow"]) + mask_s[e["d"]])
            e["kb"] = e["k"] * e["bcol"]
            e["k16"] = e["k"].astype(BF16)
        for e in ld:
            e["kk"] = lax.dot_general(e["kb"].astype(BF16), e["k16"], NT_DIMS, preferred_element_type=F32)
        for e in ld:
            e["qk"] = lax.dot_general(e["q"].astype(BF16), e["k16"], NT_DIMS, preferred_element_type=F32)
        for e in ld:
            lm = e["kk"] * e["dec"] * mask_s[2]
            e["attn"] = (e["qk"] * e["dec"]).astype(BF16)
            e["x"] = -lm
            e["l16"] = lm.astype(BF16)
        for e in ld:
            e["m"] = jnp.dot(e["l16"], e["l16"], preferred_element_type=F32)
        for it in range(n_double):
            last = it == n_double - 1
            for e in ld:
                m16 = e["m"].astype(BF16)
                x16 = e["x"].astype(BF16)
                lhs = x16 if last else jnp.concatenate([x16, m16], axis=0)
                e["both"] = jnp.dot(lhs, m16, preferred_element_type=F32)
            for e in ld:
                e["x"] = e["x"] + e["m"] + e["both"][:blk]
                if not last:
                    e["m"] = e["both"][blk:]
        for e in ld:
            e["rhs"] = jnp.concatenate([e["v"] * e["bcol"], e["kb"] * e["eg"]], axis=1)
        for e in ld:
            e["uw"] = e["rhs"] + jnp.dot(e["x"].astype(BF16), e["rhs"].astype(BF16), preferred_element_type=F32)
        for i, e in enumerate(ld):
            e["u"] = e["uw"][:, :hd]
            e["w16"] = e["uw"][:, hd:].astype(BF16)
            e["qd16"] = (e["q"] * e["eg"]).astype(BF16)
            e["kd16"] = (e["k"] * e["ekd"]).astype(BF16)
            e["state"] = st_s[i]
        for step in range(per_blk):
            for e in ld:
                ci = step if e["d"] == 0 else per_blk - 1 - step
                lo, hi = ci * c, (ci + 1) * c
                e["lo"], e["hi"] = lo, hi
                lhs = jnp.concatenate([e["w16"][lo:hi], e["qd16"][lo:hi]], axis=0)
                e["ws"] = jnp.dot(lhs, e["state"].astype(BF16), preferred_element_type=F32)
            for e in ld:
                lo, hi = e["lo"], e["hi"]
                e["vn16"] = (e["u"][lo:hi] - e["ws"][:c]).astype(BF16)
                e["av"] = jnp.dot(e["attn"][lo:hi, lo:hi], e["vn16"], preferred_element_type=F32)
            for e in ld:
                lo, hi = e["lo"], e["hi"]
                e["kv"] = lax.dot_general(e["kd16"][lo:hi], e["vn16"], TN_DIMS, preferred_element_type=F32)
            for e in ld:
                lo = e["lo"]
                e["state"] = e["state"] * e["etot"][lo:lo + 1, :] + e["kv"]
                rows = pl.ds(e["r0"] + lo, c)
                o_s[e["hh"], rows, :] = o_s[e["hh"], rows, :] + (e["ws"][c:] + e["av"])
        for i, e in enumerate(ld):
            st_s[i] = e["state"]
        return carry

    lax.fori_loop(0, n_blk, block_step, 0)

    for hh in range(hp):
        zz = z_ref[:, hh * hd:(hh + 1) * hd].astype(F32)
        y_ref[:, hh * hd:(hh + 1) * hd] = (_rms(o_s[hh], nw_ref[...]) * (zz * _sigmoid(zz))).astype(BF16)


def _deltanet(proj, ba, conv_w, gpar, norm_w, batch, seq, col0):
    hp = DN_HEADS_PER_STEP
    wblk = hp * DN_HEAD_DIM
    base = col0 * LANES // wblk
    per = DN_HEADS // hp
    t = batch * seq
    n_chain = 2 * hp
    return pl.pallas_call(
        functools.partial(_deltanet_kernel, seq=seq, hp=hp),
        grid=(batch, per),
        in_specs=[
            pl.BlockSpec((seq, wblk), lambda b, g: (b, base + g)),
            pl.BlockSpec((seq, wblk), lambda b, g: (b, base + per + g)),
            pl.BlockSpec((seq, wblk), lambda b, g: (b, base + 2 * per + g)),
            pl.BlockSpec((seq, wblk), lambda b, g: (b, base + 3 * per + g)),
            pl.BlockSpec((seq, LANES), lambda b, g: (b, g)),
            pl.BlockSpec((CONV_W, wblk), lambda b, g: (0, g)),
            pl.BlockSpec((CONV_W, wblk), lambda b, g: (0, per + g)),
            pl.BlockSpec((CONV_W, wblk), lambda b, g: (0, 2 * per + g)),
            pl.BlockSpec((None, 2, DN_GATE_ROWS, LANES), lambda b, g: (g, 0, 0, 0)),
            pl.BlockSpec((1, DN_HEAD_DIM), lambda b, g: (0, 0)),
        ],
        out_specs=pl.BlockSpec((seq, wblk), lambda b, g: (b, g)),
        out_shape=jax.ShapeDtypeStruct((t, DN_HEADS * DN_HEAD_DIM), BF16),
        scratch_shapes=[
            pltpu.VMEM((seq, LANES), F32),
            pltpu.VMEM((2, DN_GATE_ROWS, seq), F32),
            pltpu.VMEM((hp, seq, DN_HEAD_DIM), F32),
            pltpu.VMEM((hp, seq, DN_HEAD_DIM), F32),
            pltpu.VMEM((hp, seq, DN_HEAD_DIM), F32),
            pltpu.VMEM((hp, seq, DN_HEAD_DIM), F32),
            pltpu.VMEM((n_chain, DN_HEAD_DIM, DN_HEAD_DIM), F32),
            pltpu.VMEM((3, DN_BLOCK, DN_BLOCK), F32),
        ],
        compiler_params=_params(("arbitrary", "arbitrary")),
        name="deltanet",
    )(proj, proj, proj, proj, ba, conv_w, conv_w, conv_w, gpar, norm_w)


def _attn_kernel(q_ref, k_ref, v_ref, cos_ref, sin_ref, qw_ref, kw_ref, o_ref,
                 klo_s, khi_s, vlo_s, vhi_s, *, tq):
    g = pl.program_id(1)
    qi = pl.program_id(2)
    lane = lax.broadcasted_iota(jnp.int32, (1, LANES), 1)
    low = lane < ATT_HEAD_DIM
    first = (lane & (2 * ROPE_PAIRS - 1)) < ROPE_PAIRS

    def norm_rope(x, w, cos, sin):
        sq = x * x
        ms_lo = jnp.sum(jnp.where(low, sq, 0.0), axis=-1, keepdims=True)
        ms_hi = jnp.sum(jnp.where(low, 0.0, sq), axis=-1, keepdims=True)
        ms = jnp.where(low, ms_lo, ms_hi) * (1.0 / ATT_HEAD_DIM)
        y = (x * lax.rsqrt(ms + EPS)) * w
        rot = jnp.where(first, -pltpu.roll(y, LANES - ROPE_PAIRS, 1), pltpu.roll(y, ROPE_PAIRS, 1))
        return y * cos + rot * sin

    @pl.when(qi == 0)
    def _():
        in_g = (lane >= ATT_HEAD_DIM).astype(jnp.int32) == g
        is0 = g == 0
        kr = norm_rope(k_ref[...].astype(F32), kw_ref[...], cos_ref[...], sin_ref[...])
        for src, lo_s, hi_s, ones_lane in ((kr, klo_s, khi_s, None), (v_ref[...].astype(F32), vlo_s, vhi_s, True)):
            sel = jnp.where(in_g, src, 0.0)
            oth = pltpu.roll(sel, ATT_HEAD_DIM, 1)
            lo = jnp.where(is0, sel, oth)
            hi = jnp.where(is0, oth, sel)
            if ones_lane:
                lo = jnp.where(lane == ATT_HEAD_DIM, 1.0, lo)
                hi = jnp.where(lane == 0, 1.0, hi)
            lo_s[...] = lo.astype(BF16)
            hi_s[...] = hi.astype(BF16)

    r0 = pl.multiple_of(qi * tq, tq)
    cos_q = cos_ref[pl.ds(r0, tq), :]
    sin_q = sin_ref[pl.ds(r0, tq), :]
    n_jb = q_ref.shape[1] // LANES
    qbs = []
    for jb in range(n_jb):
        qb = norm_rope(q_ref[:, jb * LANES:(jb + 1) * LANES].astype(F32), qw_ref[...], cos_q, sin_q)
        qbs.append((qb * (ATT_HEAD_DIM ** -0.5 * LOG2_E)).astype(BF16))
    units = [(jb, k_s, v_s) for jb in range(n_jb) for k_s, v_s in ((klo_s, vlo_s), (khi_s, vhi_s))]
    scores = [lax.dot_general(qbs[jb], k_s[...], NT_DIMS, preferred_element_type=F32) for jb, k_s, _ in units]
    probs = [jnp.exp2(s - jnp.max(s, axis=-1, keepdims=True)).astype(BF16) for s in scores]
    outs = [jnp.dot(p, v_s[...], preferred_element_type=F32) for p, (_, _, v_s) in zip(probs, units)]
    for jb in range(n_jb):
        o_lo, o_hi = outs[2 * jb], outs[2 * jb + 1]
        o_lo = o_lo / o_lo[:, ATT_HEAD_DIM:ATT_HEAD_DIM + 1]
        o_hi = o_hi / o_hi[:, 0:1]
        o_ref[:, jb * LANES:(jb + 1) * LANES] = jnp.where(low, o_lo, o_hi).astype(BF16)


def _attention(proj, cos_t, sin_t, qw, kw, batch, seq, q_col0, k_col, v_col, tq):
    t = batch * seq
    q_w = ATT_HEADS * ATT_HEAD_DIM // ATT_KV_HEADS
    q_base = q_col0 * LANES // q_w
    n_q = seq // tq
    return pl.pallas_call(
        functools.partial(_attn_kernel, tq=tq),
        grid=(batch, ATT_KV_HEADS, n_q),
        in_specs=[
            pl.BlockSpec((tq, q_w), lambda b, g, i: (b * n_q + i, q_base + g)),
            pl.BlockSpec((seq, LANES), lambda b, g, i: (b, k_col)),
            pl.BlockSpec((seq, LANES), lambda b, g, i: (b, v_col)),
            pl.BlockSpec((seq, LANES), lambda b, g, i: (0, 0)),
            pl.BlockSpec((seq, LANES), lambda b, g, i: (0, 0)),
            pl.BlockSpec((1, LANES), lambda b, g, i: (0, 0)),
            pl.BlockSpec((1, LANES), lambda b, g, i: (0, 0)),
        ],
        out_specs=pl.BlockSpec((tq, q_w), lambda b, g, i: (b * n_q + i, g)),
        out_shape=jax.ShapeDtypeStruct((t, ATT_HEADS * ATT_HEAD_DIM), BF16),
        scratch_shapes=[pltpu.VMEM((seq, LANES), BF16) for _ in range(4)],
        compiler_params=_params(("arbitrary", "arbitrary", "arbitrary")),
        name="gqa",
    )(proj, proj, proj, cos_t, sin_t, qw, kw)


def _xattn_kernel(q_ref, mem_ref, mw_ref, wkv_ref, o_ref, mk_s, mv_s):
    xw = X_HEADS * X_HEAD_DIM

    @pl.when(pl.program_id(1) == 0)
    def _():
        hm = _rms(mem_ref[...], mw_ref[...]).astype(BF16)
        kv = jnp.dot(hm, wkv_ref[...], preferred_element_type=F32)
        mk_s[...] = kv[:, :xw].astype(BF16)
        mv_s[...] = kv[:, xw:].astype(BF16)

    for h in range(X_HEADS):
        cols = slice(h * X_HEAD_DIM, (h + 1) * X_HEAD_DIM)
        s = lax.dot_general(q_ref[:, cols], mk_s[:, cols], NT_DIMS, preferred_element_type=F32)
        s = s * (X_HEAD_DIM ** -0.5)
        p = jnp.exp(s - jnp.max(s, axis=-1, keepdims=True))
        den = jnp.sum(p, axis=-1, keepdims=True)
        o = jnp.dot(p.astype(BF16), mv_s[:, cols], preferred_element_type=F32) / den
        o_ref[:, cols] = o.astype(BF16)


def _cross_attention(proj, mem2, mem_norm_w, w_kv, batch, seq, m_len, q_col0, tm):
    t = batch * seq
    d = mem2.shape[1]
    xw = X_HEADS * X_HEAD_DIM
    q_base = q_col0 * LANES // xw
    n_t = seq // tm
    return pl.pallas_call(
        _xattn_kernel,
        grid=(batch, n_t),
        in_specs=[
            pl.BlockSpec((tm, xw), lambda b, i: (b * n_t + i, q_base)),
            pl.BlockSpec((m_len, d), lambda b, i: (b, 0)),
            pl.BlockSpec((1, d), lambda b, i: (0, 0)),
            pl.BlockSpec((d, 2 * xw), lambda b, i: (0, 0)),
        ],
        out_specs=pl.BlockSpec((tm, xw), lambda b, i: (b * n_t + i, 0)),
        out_shape=jax.ShapeDtypeStruct((t, xw), BF16),
        scratch_shapes=[pltpu.VMEM((m_len, xw), BF16), pltpu.VMEM((m_len, xw), BF16)],
        compiler_params=_params(("arbitrary", "arbitrary")),
        name="xattn",
    )(proj, mem2, mem_norm_w, w_kv)


def _merge_ffn_kernel(x_ref, g_ref, ydn_ref, yatt_ref, yx_ref, wb_ref, wo_ref, nw_ref, wu_ref, wd_ref, fw_ref,
                      o_ref, *, ff_chunk, final_norm):
    d = x_ref.shape[1]
    merged = None
    for n, y_ref in enumerate((ydn_ref, yatt_ref, yx_ref)):
        yb = jnp.dot(y_ref[...], wb_ref[n], preferred_element_type=F32)
        term = _sigmoid(g_ref[:, n * d:(n + 1) * d].astype(F32)) * yb
        merged = term if merged is None else merged + term
    x = x_ref[...] + jnp.dot(merged.astype(BF16), wo_ref[...], preferred_element_type=F32)
    h = _rms(x, nw_ref[...]).astype(BF16)
    acc = x
    for c0 in range(0, wu_ref.shape[1], ff_chunk):
        u = jnp.dot(h, wu_ref[:, c0:c0 + ff_chunk], preferred_element_type=F32)
        a = jnp.square(jnp.maximum(u, 0.0)).astype(BF16)
        acc = acc + jnp.dot(a, wd_ref[c0:c0 + ff_chunk, :], preferred_element_type=F32)
    o_ref[...] = _rms(acc, fw_ref[...]) if final_norm else acc


def _merge_ffn(x2, proj, ydn, yatt, yx, w_branch, w_out, norm_w, w_up, w_down, final_w, final_norm, tm):
    t, d = x2.shape
    bw = ydn.shape[1]
    dff = w_up.shape[1]
    resident = dict(pipeline_mode=pl.Buffered(1))
    return pl.pallas_call(
        functools.partial(_merge_ffn_kernel, ff_chunk=1024, final_norm=final_norm),
        grid=(t // tm,),
        in_specs=[
            pl.BlockSpec((tm, d), lambda i: (i, 0)),
            pl.BlockSpec((tm, N_BRANCH * d), lambda i: (i, 0)),
            pl.BlockSpec((tm, bw), lambda i: (i, 0)),
            pl.BlockSpec((tm, bw), lambda i: (i, 0)),
            pl.BlockSpec((tm, bw), lambda i: (i, 0)),
            pl.BlockSpec((N_BRANCH, bw, d), lambda i: (0, 0, 0), **resident),
            pl.BlockSpec((d, d), lambda i: (0, 0), **resident),
            pl.BlockSpec((1, d), lambda i: (0, 0)),
            pl.BlockSpec((d, dff), lambda i: (0, 0), **resident),
            pl.BlockSpec((dff, d), lambda i: (0, 0), **resident),
            pl.BlockSpec((1, d), lambda i: (0, 0)),
        ],
        out_specs=pl.BlockSpec((tm, d), lambda i: (i, 0)),
        out_shape=jax.ShapeDtypeStruct((t, d), F32),
        compiler_params=_params(("arbitrary",)),
        name="merge_ffn",
    )(x2, proj, ydn, yatt, yx, w_branch, w_out, norm_w, w_up, w_down, final_w)


def _gate_lanes(beta_like, alpha_like):
    hp = DN_HEADS_PER_STEP
    lead = beta_like.shape[:-1]
    blocks = []
    for g in range(DN_GROUPS):
        pick = lambda a: [a[..., dd * DN_HEADS + g * hp: dd * DN_HEADS + (g + 1) * hp] for dd in range(2)]
        blocks += pick(beta_like) + [jnp.zeros(lead + (DN_ALPHA_LANE - 2 * hp,), beta_like.dtype)]
        blocks += pick(alpha_like) + [jnp.zeros(lead + (LANES - DN_ALPHA_LANE - 2 * hp,), beta_like.dtype)]
    return jnp.concatenate(blocks, axis=-1)


def _rope_tables(seq):
    pos = jnp.arange(seq)
    inv_freq = ROPE_THETA ** (-jnp.arange(ROPE_PAIRS, dtype=F32) / ROPE_PAIRS)
    ang_r = (pos // GRID_W).astype(F32)[:, None] * inv_freq
    ang_c = (pos % GRID_W).astype(F32)[:, None] * inv_freq
    ang = jnp.concatenate([ang_r, ang_r, ang_c, ang_c] * (LANES // ATT_HEAD_DIM), axis=1)
    return jnp.cos(ang), jnp.sin(ang)


def kernel(x, mem, mix_norm_w, w_in, dn_conv_w, dn_a_log, dn_dt_bias, dn_norm_w, q_norm_w, k_norm_w,
           mem_norm_w, w_mem_kv, w_branch, w_out, ffn_norm_w, w_up, w_down, final_norm_w):
    batch, seq, d = x.shape
    m_len = mem.shape[1]
    depth = w_in.shape[0]
    bw = d // 2
    dn_cols = 3 * bw
    assert seq % DN_BLOCK == 0 and seq % 512 == 0 and DN_HEADS * DN_HEAD_DIM == bw
    assert ATT_HEADS * ATT_HEAD_DIM == bw and X_HEADS * X_HEAD_DIM == bw

    n_gate = N_BRANCH * d
    col_dn = n_gate // LANES
    col_aq = col_dn + (dn_cols + bw) // LANES
    col_xq = col_aq + bw // LANES
    col_ak = col_xq + bw // LANES
    col_av = col_ak + 1
    n_proj = (col_av + 1) * LANES
    o_qkv, o_z, o_b, o_a = 0, dn_cols, dn_cols + bw, dn_cols + bw + 2 * DN_HEADS
    o_aq = o_a + 2 * DN_HEADS
    o_ak = o_aq + bw
    o_av = o_ak + ATT_KV_HEADS * ATT_HEAD_DIM
    o_xq = o_av + ATT_KV_HEADS * ATT_HEAD_DIM
    o_g = o_xq + bw

    cos_t, sin_t = _rope_tables(seq)
    x2 = x.reshape(batch * seq, d)
    mem2 = mem.reshape(batch * m_len, d)
    tile = lambda w: jnp.tile(w, LANES // ATT_HEAD_DIM)[None, :]

    for l in range(depth):
        wl = w_in[l].astype(BF16)
        w_prep = jnp.concatenate(
            [wl[:, o_g:], wl[:, o_qkv:o_b], wl[:, o_aq:o_ak], wl[:, o_xq:o_g], wl[:, o_ak:o_xq],
             _gate_lanes(wl[:, o_b:o_a], wl[:, o_a:o_aq])], axis=1)
        zeros8 = jnp.zeros((2 * DN_HEADS,), F32)
        gate_rows = lambda p: jnp.broadcast_to(
            _gate_lanes(zeros8, p.reshape(-1)).reshape(DN_GROUPS, 1, LANES)[:, :, :DN_GATE_ROWS, None],
            (DN_GROUPS, 1, DN_GATE_ROWS, LANES))
        gpar = jnp.concatenate([gate_rows(dn_a_log[l]), gate_rows(dn_dt_bias[l])], axis=1)

        proj, ba = _inproj(x2, mix_norm_w[l][None, :], w_prep, n_proj, tm=512)
        ydn = _deltanet(proj, ba, dn_conv_w[l], gpar, dn_norm_w[l][None, :], batch, seq, col_dn)
        yatt = _attention(proj, cos_t, sin_t, tile(q_norm_w[l]), tile(k_norm_w[l]), batch, seq,
                          col_aq, col_ak, col_av, tq=256)
        yx = _cross_attention(proj, mem2, mem_norm_w[l][None, :], w_mem_kv[l].astype(BF16),
                              batch, seq, m_len, col_xq, tm=512)
        x2 = _merge_ffn(x2, proj, ydn, yatt, yx, w_branch[l].astype(BF16), w_out[l].astype(BF16),
                        ffn_norm_w[l][None, :], w_up[l].astype(BF16), w_down[l].astype(BF16),
                        final_norm_w[None, :], final_norm=(l == depth - 1), tm=512)
    return x2.reshape(batch, seq, d)
```

```python
import functools

import jax
import jax.numpy as jnp
from jax import lax
from jax.experimental import pallas as pl
from jax.experimental.pallas import tpu as pltpu

F32 = jnp.float32
BF16 = jnp.bfloat16

DN_HEADS = 4
DN_HEAD_DIM = 128
CONV_W = 5
ATT_HEADS = 8
ATT_KV_HEADS = 2
ATT_HEAD_DIM = 64
ROPE_PAIRS = ATT_HEAD_DIM // 4
ROPE_THETA = 10000.0
GRID_W = 64
X_HEADS = 4
X_HEAD_DIM = 128
N_BRANCH = 3
EPS = 1e-6
LOG2_E = 1.4426950408889634

LANES = 128
V7X_VMEM_LIMIT = 56 * 1024 * 1024

DN_CHUNK = 64
DN_BLOCK = 256
DN_HEADS_PER_STEP = 2
DN_GROUPS = DN_HEADS // DN_HEADS_PER_STEP
DN_ALPHA_LANE = 8
DN_GATE_ROWS = 2 * DN_ALPHA_LANE
DN_COL_BETA, DN_COL_GC, DN_COL_TOT, DN_COL_EG, DN_COL_EKD, DN_COL_ETOT = 0, 1, 3, 4, 6, 8

NT_DIMS = (((1,), (1,)), ((), ()))
TN_DIMS = (((0,), (0,)), ((), ()))
NEG_BIG = -1e30


def _rms(x, w):
    return (x * lax.rsqrt(jnp.mean(x * x, axis=-1, keepdims=True) + EPS)) * w


def _sigmoid(x):
    return 1.0 / (1.0 + jnp.exp(-x))


def _softplus(x):
    return jnp.maximum(x, 0.0) + jnp.log(1.0 + jnp.exp(-jnp.abs(x)))


def _params(semantics):
    return pltpu.CompilerParams(dimension_semantics=semantics, vmem_limit_bytes=V7X_VMEM_LIMIT)


def _head_norm_rope(x, w, cos, sin):
    lane = lax.broadcasted_iota(jnp.int32, (1, LANES), 1)
    low = lane < ATT_HEAD_DIM
    first = (lane & (2 * ROPE_PAIRS - 1)) < ROPE_PAIRS
    sq = x * x
    ms_lo = jnp.sum(jnp.where(low, sq, 0.0), axis=-1, keepdims=True)
    ms_hi = jnp.sum(jnp.where(low, 0.0, sq), axis=-1, keepdims=True)
    ms = jnp.where(low, ms_lo, ms_hi) * (1.0 / ATT_HEAD_DIM)
    y = (x * lax.rsqrt(ms + EPS)) * w
    rot = jnp.where(first, -pltpu.roll(y, LANES - ROPE_PAIRS, 1), pltpu.roll(y, ROPE_PAIRS, 1))
    return y * cos + rot * sin


def _inproj_kernel(x_ref, nw_ref, w_ref, cos_ref, sin_ref, qw_ref, kw_ref, proj_ref, ba_ref, *,
                   n_proj, col_chunk, q_cols, k_cols):
    h = _rms(x_ref[...], nw_ref[...]).astype(BF16)
    has_rope = lambda c0: any(max(c0, lo) < min(c0 + col_chunk, hi) for lo, hi in (q_cols, k_cols))
    for c0 in sorted(range(0, n_proj, col_chunk), key=lambda c0: not has_rope(c0)):
        c1 = min(c0 + col_chunk, n_proj)
        acc = jnp.dot(h, w_ref[:, c0:c1], preferred_element_type=F32)
        for b0 in range(c0, c1, LANES):
            blk = acc[:, b0 - c0:b0 - c0 + LANES]
            if q_cols[0] <= b0 < q_cols[1]:
                blk = _head_norm_rope(blk, qw_ref[...], cos_ref[...], sin_ref[...]) * (ATT_HEAD_DIM ** -0.5 * LOG2_E)
            elif k_cols[0] <= b0 < k_cols[1]:
                blk = _head_norm_rope(blk, kw_ref[...], cos_ref[...], sin_ref[...])
            proj_ref[:, b0:b0 + LANES] = blk.astype(BF16)
    ba_ref[...] = jnp.dot(h, w_ref[:, n_proj:], preferred_element_type=F32)


def _inproj(x2, norm_w, w_prep, cos_t, sin_t, qw, kw, n_proj, q_cols, k_cols, tm):
    t, d = x2.shape
    n_all = w_prep.shape[1]
    n_pos = cos_t.shape[0] // tm
    return pl.pallas_call(
        functools.partial(_inproj_kernel, n_proj=n_proj, col_chunk=512, q_cols=q_cols, k_cols=k_cols),
        grid=(t // tm,),
        in_specs=[
            pl.BlockSpec((tm, d), lambda i: (i, 0)),
            pl.BlockSpec((1, d), lambda i: (0, 0)),
            pl.BlockSpec((d, n_all), lambda i: (0, 0)),
            pl.BlockSpec((tm, LANES), lambda i: (i % n_pos, 0)),
            pl.BlockSpec((tm, LANES), lambda i: (i % n_pos, 0)),
            pl.BlockSpec((1, LANES), lambda i: (0, 0)),
            pl.BlockSpec((1, LANES), lambda i: (0, 0)),
        ],
        out_specs=[
            pl.BlockSpec((tm, n_proj), lambda i: (i, 0)),
            pl.BlockSpec((tm, n_all - n_proj), lambda i: (i, 0)),
        ],
        out_shape=[
            jax.ShapeDtypeStruct((t, n_proj), BF16),
            jax.ShapeDtypeStruct((t, n_all - n_proj), F32),
        ],
        compiler_params=_params(("arbitrary",)),
        name="inproj",
    )(x2, norm_w, w_prep, cos_t, sin_t, qw, kw)


def _deltanet_kernel(dq_ref, dk_ref, dv_ref, z_ref, ba_ref, wq_ref, wk_ref, wv_ref, gpar_ref, nw_ref,
                     y_ref, col_s, row_s, q_s, k_s, v_s, o_s, st_s, mask_s, bd_s, *pipe_s, seq, hp):
    c = DN_CHUNK
    blk = DN_BLOCK
    n_blk = seq // blk
    per_blk = blk // c
    hd = DN_HEAD_DIM

    gates = ba_ref[...].T[0:DN_GATE_ROWS, :]
    beta = _sigmoid(gates)
    g = -jnp.exp(gpar_ref[0][:, 0:1]) * _softplus(gates + gpar_ref[1][:, 0:1])
    lin = lax.broadcasted_iota(jnp.int32, (DN_GATE_ROWS, seq), 1) & (c - 1)
    pre = g
    suf = g
    sh = 1
    while sh < c:
        pre = pre + jnp.where(lin >= sh, pltpu.roll(pre, sh, 1), 0.0)
        suf = suf + jnp.where(lin < c - sh, pltpu.roll(suf, seq - sh, 1), 0.0)
        sh *= 2
    tot = pre + suf - g
    row_s[0] = pre
    row_s[1] = suf
    a0 = DN_ALPHA_LANE
    packed = [beta[0:a0], pre[a0:], suf[a0:], tot[a0:], jnp.exp(pre[a0:]), jnp.exp(suf[a0:]),
              jnp.exp(tot[a0:] - pre[a0:]), jnp.exp(tot[a0:] - suf[a0:]), jnp.exp(tot[a0:])]
    packed.append(jnp.zeros((LANES - a0 * len(packed), seq), F32))
    col_s[...] = jnp.concatenate(packed, axis=0).T

    ii = lax.broadcasted_iota(jnp.int32, (c, blk), 0)
    jj = lax.broadcasted_iota(jnp.int32, (c, blk), 1) & (c - 1)
    mask_s[0] = jnp.where(ii >= jj, 0.0, NEG_BIG)
    mask_s[1] = jnp.where(ii <= jj, 0.0, NEG_BIG)
    mask_s[2] = jnp.where(ii != jj, 1.0, 0.0)
    bi = lax.broadcasted_iota(jnp.int32, (blk, blk), 0) // c
    bj = lax.broadcasted_iota(jnp.int32, (blk, blk), 1) // c
    bd_s[...] = jnp.where(bi == bj, 1.0, 0.0).astype(BF16)
    lane_chunk = lax.broadcasted_iota(jnp.int32, (1, blk), 1) // c

    tpos = lax.broadcasted_iota(jnp.int32, (seq, hd), 0)

    def conv_silu(x_ref, w_ref, hh):
        x = x_ref[:, hh * hd:(hh + 1) * hd].astype(F32)
        w = w_ref[:, hh * hd:(hh + 1) * hd]
        y = x * w[2:3, :]
        y = y + jnp.where(tpos >= 2, pltpu.roll(x, 2, 0), 0.0) * w[0:1, :]
        y = y + jnp.where(tpos >= 1, pltpu.roll(x, 1, 0), 0.0) * w[1:2, :]
        y = y + jnp.where(tpos < seq - 1, pltpu.roll(x, seq - 1, 0), 0.0) * w[3:4, :]
        y = y + jnp.where(tpos < seq - 2, pltpu.roll(x, seq - 2, 0), 0.0) * w[4:5, :]
        return y * _sigmoid(y)

    def l2n(x):
        return x * lax.rsqrt(jnp.sum(x * x, axis=-1, keepdims=True) + EPS)

    for hh in range(hp):
        q_s[hh] = l2n(conv_silu(dq_ref, wq_ref, hh)) * (hd ** -0.5)
        k_s[hh] = l2n(conv_silu(dk_ref, wk_ref, hh))
        v_s[hh] = conv_silu(dv_ref, wv_ref, hh)
    o_s[...] = jnp.zeros(o_s.shape, F32)
    st_s[...] = jnp.zeros(st_s.shape, F32)

    n_double = c.bit_length() - 2

    chains = [(hh, d) for hh in range(hp) for d in range(2)]

    def diag_wide(a):
        w = a[(per_blk - 1) * c:]
        for i in range(per_blk - 2, -1, -1):
            w = jnp.where(lane_chunk == i, a[i * c:(i + 1) * c], w)
        return w

    def to_blockdiag16(w):
        return jnp.concatenate([w.astype(BF16)] * per_blk, axis=0) * bd_s[...]

    def block_rows(r, d):
        return pl.multiple_of((r if d == 0 else n_blk - 1 - r) * blk, blk)

    def prescan_stages(r, slot):
        pu, pw, pq, pk, pa = slot
        ld = []

        def load():
            for hh, d in chains:
                r0 = block_rows(r, d)
                ci = d * hp + hh
                cols = col_s[pl.ds(r0, blk), :]
                col = lambda group: cols[:, DN_ALPHA_LANE * group + ci:DN_ALPHA_LANE * group + ci + 1]
                e = dict(d=d, q=q_s[hh, pl.ds(r0, blk), :], k=k_s[hh, pl.ds(r0, blk), :],
                         v=v_s[hh, pl.ds(r0, blk), :], bcol=col(DN_COL_BETA), gcol=col(DN_COL_GC + d),
                         eg=col(DN_COL_EG + d), ekd=col(DN_COL_EKD + d),
                         grow=row_s[d, DN_ALPHA_LANE + ci:DN_ALPHA_LANE + ci + 1, pl.ds(r0, blk)])
                e["dec"] = jnp.exp((diag_wide(e["gcol"]) - e["grow"]) + mask_s[d])
                e["kb"] = e["k"] * e["bcol"]
                e["k16"] = e["k"].astype(BF16)
                ld.append(e)

        def kk_mm():
            for e in ld:
                e["kk"] = lax.dot_general(e["kb"].astype(BF16), e["k16"], NT_DIMS, preferred_element_type=F32)

        def qk_mm():
            for e in ld:
                e["qk"] = lax.dot_general(e["q"].astype(BF16), e["k16"], NT_DIMS, preferred_element_type=F32)

        def masked():
            for i, e in enumerate(ld):
                e["l_w"] = diag_wide(e["kk"]) * e["dec"] * mask_s[2]
                pa[i] = (diag_wide(e["qk"]) * e["dec"]).astype(BF16)

        def square():
            for e in ld:
                e["x"] = -e["l_w"]
                e["m"] = jnp.dot(e["l_w"].astype(BF16), to_blockdiag16(e["l_w"]), preferred_element_type=F32)

        def double_mm(last):
            for e in ld:
                x16 = e["x"].astype(BF16)
                lhs = x16 if last else jnp.concatenate([x16, e["m"].astype(BF16)], axis=0)
                e["both"] = jnp.dot(lhs, to_blockdiag16(e["m"]), preferred_element_type=F32)

        def double_add(last):
            for e in ld:
                e["x"] = e["x"] + e["m"] + e["both"][:c]
                if not last:
                    e["m"] = e["both"][c:]

        def make_rhs():
            for e in ld:
                e["rhs"] = jnp.concatenate([e["v"] * e["bcol"], e["kb"] * e["eg"]], axis=1)

        def uw_mm():
            for e in ld:
                e["uw"] = e["rhs"] + jnp.dot(to_blockdiag16(e["x"]), e["rhs"].astype(BF16),
                                             preferred_element_type=F32)

        def store():
            for i, e in enumerate(ld):
                pu[i] = e["uw"][:, :hd]
                pw[i] = e["uw"][:, hd:].astype(BF16)
                pq[i] = (e["q"] * e["eg"]).astype(BF16)
                pk[i] = (e["k"] * e["ekd"]).astype(BF16)

        stages = [load, kk_mm, qk_mm, masked, square]
        for it in range(n_double):
            last = it == n_double - 1
            stages += [functools.partial(double_mm, last), functools.partial(double_add, last)]
        return stages + [make_rhs, uw_mm, store]

    def scan_stages(r, slot):
        pu, pw, pq, pk, pa = slot
        ld = []

        def load():
            for i, (hh, d) in enumerate(chains):
                r0 = block_rows(r, d)
                lane = DN_ALPHA_LANE * DN_COL_ETOT + d * hp + hh
                ld.append(dict(i=i, hh=hh, d=d, r0=r0, etot=col_s[pl.ds(r0, blk), :][:, lane:lane + 1],
                               state=st_s[i]))

        def ws_mm(step):
            for e in ld:
                ci = step if e["d"] == 0 else per_blk - 1 - step
                lo, hi = ci * c, (ci + 1) * c
                e["lo"], e["hi"] = lo, hi
                lhs = jnp.concatenate([pw[e["i"], lo:hi, :], pq[e["i"], lo:hi, :]], axis=0)
                e["ws"] = jnp.dot(lhs, e["state"].astype(BF16), preferred_element_type=F32)

        def av_mm(step):
            for e in ld:
                lo, hi = e["lo"], e["hi"]
                e["vn16"] = (pu[e["i"], lo:hi, :] - e["ws"][:c]).astype(BF16)
                e["av"] = jnp.dot(pa[e["i"]][:, lo:hi], e["vn16"], preferred_element_type=F32)

        def kv_mm(step):
            for e in ld:
                lo, hi = e["lo"], e["hi"]
                e["kv"] = lax.dot_general(pk[e["i"], lo:hi, :], e["vn16"], TN_DIMS, preferred_element_type=F32)

        def update(step):
            for e in ld:
                lo = e["lo"]
                e["state"] = e["state"] * e["etot"][lo:lo + 1, :] + e["kv"]
                rows = pl.ds(e["r0"] + lo, c)
                o_s[e["hh"], rows, :] = o_s[e["hh"], rows, :] + (e["ws"][c:] + e["av"])

        def store():
            for e in ld:
                st_s[e["i"]] = e["state"]

        stages = [load]
        for step in range(per_blk):
            stages += [functools.partial(f, step) for f in (ws_mm, av_mm, kv_mm, update)]
        return stages + [store]

    def run_interleaved(*stage_lists):
        for k in range(max(len(s) for s in stage_lists)):
            for s in stage_lists:
                if k < len(s):
                    s[k]()

    slot_a, slot_b = pipe_s[:5], pipe_s[5:]
    run_interleaved(prescan_stages(0, slot_a))

    def two_block_steps(j, carry):
        r = 2 * j
        run_interleaved(prescan_stages(r + 1, slot_b), scan_stages(r, slot_a))
        run_interleaved(prescan_stages(jnp.minimum(r + 2, n_blk - 1), slot_a), scan_stages(r + 1, slot_b))
        return carry

    lax.fori_loop(0, n_blk // 2, two_block_steps, 0)

    for hh in range(hp):
        zz = z_ref[:, hh * hd:(hh + 1) * hd].astype(F32)
        y_ref[:, hh * hd:(hh + 1) * hd] = (_rms(o_s[hh], nw_ref[...]) * (zz * _sigmoid(zz))).astype(BF16)


def _deltanet(proj, ba, conv_w, gpar, norm_w, batch, seq, col0):
    hp = DN_HEADS_PER_STEP
    wblk = hp * DN_HEAD_DIM
    base = col0 * LANES // wblk
    per = DN_HEADS // hp
    t = batch * seq
    n_chain = 2 * hp
    return pl.pallas_call(
        functools.partial(_deltanet_kernel, seq=seq, hp=hp),
        grid=(batch, per),
        in_specs=[
            pl.BlockSpec((seq, wblk), lambda b, g: (b, base + g)),
            pl.BlockSpec((seq, wblk), lambda b, g: (b, base + per + g)),
            pl.BlockSpec((seq, wblk), lambda b, g: (b, base + 2 * per + g)),
            pl.BlockSpec((seq, wblk), lambda b, g: (b, base + 3 * per + g)),
            pl.BlockSpec((seq, LANES), lambda b, g: (b, g)),
            pl.BlockSpec((CONV_W, wblk), lambda b, g: (0, g)),
            pl.BlockSpec((CONV_W, wblk), lambda b, g: (0, per + g)),
            pl.BlockSpec((CONV_W, wblk), lambda b, g: (0, 2 * per + g)),
            pl.BlockSpec((None, 2, DN_GATE_ROWS, LANES), lambda b, g: (g, 0, 0, 0)),
            pl.BlockSpec((1, DN_HEAD_DIM), lambda b, g: (0, 0)),
        ],
        out_specs=pl.BlockSpec((seq, wblk), lambda b, g: (b, g)),
        out_shape=jax.ShapeDtypeStruct((t, DN_HEADS * DN_HEAD_DIM), BF16),
        scratch_shapes=[
            pltpu.VMEM((seq, LANES), F32),
            pltpu.VMEM((2, DN_GATE_ROWS, seq), F32),
            pltpu.VMEM((hp, seq, DN_HEAD_DIM), F32),
            pltpu.VMEM((hp, seq, DN_HEAD_DIM), F32),
            pltpu.VMEM((hp, seq, DN_HEAD_DIM), F32),
            pltpu.VMEM((hp, seq, DN_HEAD_DIM), F32),
            pltpu.VMEM((n_chain, DN_HEAD_DIM, DN_HEAD_DIM), F32),
            pltpu.VMEM((3, DN_CHUNK, DN_BLOCK), F32),
            pltpu.VMEM((DN_BLOCK, DN_BLOCK), BF16),
        ] + 2 * [
            pltpu.VMEM((n_chain, DN_BLOCK, DN_HEAD_DIM), F32),
            pltpu.VMEM((n_chain, DN_BLOCK, DN_HEAD_DIM), BF16),
            pltpu.VMEM((n_chain, DN_BLOCK, DN_HEAD_DIM), BF16),
            pltpu.VMEM((n_chain, DN_BLOCK, DN_HEAD_DIM), BF16),
            pltpu.VMEM((n_chain, DN_CHUNK, DN_BLOCK), BF16),
        ],
        compiler_params=_params(("arbitrary", "arbitrary")),
        name="deltanet",
    )(proj, proj, proj, proj, ba, conv_w, conv_w, conv_w, gpar, norm_w)


def _attn_kernel(q_ref, k_ref, v_ref, o_ref, klo_s, khi_s, vlo_s, vhi_s):
    g = pl.program_id(1)
    lane = lax.broadcasted_iota(jnp.int32, (1, LANES), 1)
    low = lane < ATT_HEAD_DIM

    @pl.when(pl.program_id(2) == 0)
    def _():
        in_g = (lane >= ATT_HEAD_DIM).astype(jnp.int32) == g
        is0 = g == 0
        for src_ref, lo_s, hi_s, ones_lane in ((k_ref, klo_s, khi_s, None), (v_ref, vlo_s, vhi_s, True)):
            sel = jnp.where(in_g, src_ref[...].astype(F32), 0.0)
            oth = pltpu.roll(sel, ATT_HEAD_DIM, 1)
            lo = jnp.where(is0, sel, oth)
            hi = jnp.where(is0, oth, sel)
            if ones_lane:
                lo = jnp.where(lane == ATT_HEAD_DIM, 1.0, lo)
                hi = jnp.where(lane == 0, 1.0, hi)
            lo_s[...] = lo.astype(BF16)
            hi_s[...] = hi.astype(BF16)

    n_jb = q_ref.shape[1] // LANES
    qbs = [q_ref[:, jb * LANES:(jb + 1) * LANES] for jb in range(n_jb)]
    units = [(jb, k_s, v_s) for jb in range(n_jb) for k_s, v_s in ((klo_s, vlo_s), (khi_s, vhi_s))]
    scores = [lax.dot_general(qbs[jb], k_s[...], NT_DIMS, preferred_element_type=F32) for jb, k_s, _ in units]
    probs = [jnp.exp2(s - jnp.max(s, axis=-1, keepdims=True)).astype(BF16) for s in scores]
    outs = [jnp.dot(p, v_s[...], preferred_element_type=F32) for p, (_, _, v_s) in zip(probs, units)]
    for jb in range(n_jb):
        o_lo, o_hi = outs[2 * jb], outs[2 * jb + 1]
        o_lo = o_lo / o_lo[:, ATT_HEAD_DIM:ATT_HEAD_DIM + 1]
        o_hi = o_hi / o_hi[:, 0:1]
        o_ref[:, jb * LANES:(jb + 1) * LANES] = jnp.where(low, o_lo, o_hi).astype(BF16)


def _attention(proj, batch, seq, q_col0, k_col, v_col, tq):
    t = batch * seq
    q_w = ATT_HEADS * ATT_HEAD_DIM // ATT_KV_HEADS
    q_base = q_col0 * LANES // q_w
    n_q = seq // tq
    return pl.pallas_call(
        _attn_kernel,
        grid=(batch, ATT_KV_HEADS, n_q),
        in_specs=[
            pl.BlockSpec((tq, q_w), lambda b, g, i: (b * n_q + i, q_base + g)),
            pl.BlockSpec((seq, LANES), lambda b, g, i: (b, k_col)),
            pl.BlockSpec((seq, LANES), lambda b, g, i: (b, v_col)),
        ],
        out_specs=pl.BlockSpec((tq, q_w), lambda b, g, i: (b * n_q + i, g)),
        out_shape=jax.ShapeDtypeStruct((t, ATT_HEADS * ATT_HEAD_DIM), BF16),
        scratch_shapes=[pltpu.VMEM((seq, LANES), BF16) for _ in range(4)],
        compiler_params=_params(("arbitrary", "arbitrary", "arbitrary")),
        name="gqa",
    )(proj, proj, proj)


def _xattn_kernel(q_ref, mem_ref, mw_ref, wkv_ref, o_ref, mk_s, mv_s):
    xw = X_HEADS * X_HEAD_DIM

    @pl.when(pl.program_id(1) == 0)
    def _():
        hm = _rms(mem_ref[...], mw_ref[...]).astype(BF16)
        kv = jnp.dot(hm, wkv_ref[...], preferred_element_type=F32)
        mk_s[...] = kv[:, :xw].astype(BF16)
        mv_s[...] = kv[:, xw:].astype(BF16)

    heads = [slice(h * X_HEAD_DIM, (h + 1) * X_HEAD_DIM) for h in range(X_HEADS)]
    scores = [lax.dot_general(q_ref[:, cols], mk_s[:, cols], NT_DIMS, preferred_element_type=F32)
              * (X_HEAD_DIM ** -0.5 * LOG2_E) for cols in heads]
    probs = [jnp.exp2(s - jnp.max(s, axis=-1, keepdims=True)) for s in scores]
    dens = [jnp.sum(p, axis=-1, keepdims=True) for p in probs]
    outs = [jnp.dot(p.astype(BF16), mv_s[:, cols], preferred_element_type=F32) for p, cols in zip(probs, heads)]
    for o, den, cols in zip(outs, dens, heads):
        o_ref[:, cols] = (o / den).astype(BF16)


def _cross_attention(proj, mem2, mem_norm_w, w_kv, batch, seq, m_len, q_col0, tm):
    t = batch * seq
    d = mem2.shape[1]
    xw = X_HEADS * X_HEAD_DIM
    q_base = q_col0 * LANES // xw
    n_t = seq // tm
    return pl.pallas_call(
        _xattn_kernel,
        grid=(batch, n_t),
        in_specs=[
            pl.BlockSpec((tm, xw), lambda b, i: (b * n_t + i, q_base)),
            pl.BlockSpec((m_len, d), lambda b, i: (b, 0)),
            pl.BlockSpec((1, d), lambda b, i: (0, 0)),
            pl.BlockSpec((d, 2 * xw), lambda b, i: (0, 0)),
        ],
        out_specs=pl.BlockSpec((tm, xw), lambda b, i: (b * n_t + i, 0)),
        out_shape=jax.ShapeDtypeStruct((t, xw), BF16),
        scratch_shapes=[pltpu.VMEM((m_len, xw), BF16), pltpu.VMEM((m_len, xw), BF16)],
        compiler_params=_params(("arbitrary", "arbitrary")),
        name="xattn",
    )(proj, mem2, mem_norm_w, w_kv)


def _merge_ffn_kernel(x_ref, g_ref, ydn_ref, yatt_ref, yx_ref, wb_ref, wo_ref, nw_ref, wu_ref, wd_ref, fw_ref,
                      o_ref, *, ff_chunk, final_norm):
    d = x_ref.shape[1]
    merged = None
    for n, y_ref in enumerate((ydn_ref, yatt_ref, yx_ref)):
        yb = jnp.dot(y_ref[...], wb_ref[n], preferred_element_type=F32)
        term = _sigmoid(g_ref[:, n * d:(n + 1) * d].astype(F32)) * yb
        merged = term if merged is None else merged + term
    x = x_ref[...] + jnp.dot(merged.astype(BF16), wo_ref[...], preferred_element_type=F32)
    h = _rms(x, nw_ref[...]).astype(BF16)
    acc = x
    for c0 in range(0, wu_ref.shape[1], ff_chunk):
        u = jnp.dot(h, wu_ref[:, c0:c0 + ff_chunk], preferred_element_type=F32)
        a = jnp.square(jnp.maximum(u, 0.0)).astype(BF16)
        acc = acc + jnp.dot(a, wd_ref[c0:c0 + ff_chunk, :], preferred_element_type=F32)
    o_ref[...] = _rms(acc, fw_ref[...]) if final_norm else acc


def _merge_ffn(x2, proj, ydn, yatt, yx, w_branch, w_out, norm_w, w_up, w_down, final_w, final_norm, tm):
    t, d = x2.shape
    bw = ydn.shape[1]
    dff = w_up.shape[1]
    resident = dict(pipeline_mode=pl.Buffered(1))
    return pl.pallas_call(
        functools.partial(_merge_ffn_kernel, ff_chunk=1024, final_norm=final_norm),
        grid=(t // tm,),
        in_specs=[
            pl.BlockSpec((tm, d), lambda i: (i, 0)),
            pl.BlockSpec((tm, N_BRANCH * d), lambda i: (i, 0)),
            pl.BlockSpec((tm, bw), lambda i: (i, 0)),
            pl.BlockSpec((tm, bw), lambda i: (i, 0)),
            pl.BlockSpec((tm, bw), lambda i: (i, 0)),
            pl.BlockSpec((N_BRANCH, bw, d), lambda i: (0, 0, 0), **resident),
            pl.BlockSpec((d, d), lambda i: (0, 0), **resident),
            pl.BlockSpec((1, d), lambda i: (0, 0)),
            pl.BlockSpec((d, dff), lambda i: (0, 0), **resident),
            pl.BlockSpec((dff, d), lambda i: (0, 0), **resident),
            pl.BlockSpec((1, d), lambda i: (0, 0)),
        ],
        out_specs=pl.BlockSpec((tm, d), lambda i: (i, 0)),
        out_shape=jax.ShapeDtypeStruct((t, d), F32),
        compiler_params=_params(("arbitrary",)),
        name="merge_ffn",
    )(x2, proj, ydn, yatt, yx, w_branch, w_out, norm_w, w_up, w_down, final_w)


def _gate_lanes(beta_like, alpha_like):
    hp = DN_HEADS_PER_STEP
    lead = beta_like.shape[:-1]
    blocks = []
    for g in range(DN_GROUPS):
        pick = lambda a: [a[..., dd * DN_HEADS + g * hp: dd * DN_HEADS + (g + 1) * hp] for dd in range(2)]
        blocks += pick(beta_like) + [jnp.zeros(lead + (DN_ALPHA_LANE - 2 * hp,), beta_like.dtype)]
        blocks += pick(alpha_like) + [jnp.zeros(lead + (LANES - DN_ALPHA_LANE - 2 * hp,), beta_like.dtype)]
    return jnp.concatenate(blocks, axis=-1)


def _rope_tables(seq):
    pos = jnp.arange(seq)
    inv_freq = ROPE_THETA ** (-jnp.arange(ROPE_PAIRS, dtype=F32) / ROPE_PAIRS)
    ang_r = (pos // GRID_W).astype(F32)[:, None] * inv_freq
    ang_c = (pos % GRID_W).astype(F32)[:, None] * inv_freq
    ang = jnp.concatenate([ang_r, ang_r, ang_c, ang_c] * (LANES // ATT_HEAD_DIM), axis=1)
    return jnp.cos(ang), jnp.sin(ang)


def kernel(x, mem, mix_norm_w, w_in, dn_conv_w, dn_a_log, dn_dt_bias, dn_norm_w, q_norm_w, k_norm_w,
           mem_norm_w, w_mem_kv, w_branch, w_out, ffn_norm_w, w_up, w_down, final_norm_w):
    batch, seq, d = x.shape
    m_len = mem.shape[1]
    depth = w_in.shape[0]
    bw = d // 2
    dn_cols = 3 * bw
    assert seq % DN_BLOCK == 0 and seq % 512 == 0 and DN_HEADS * DN_HEAD_DIM == bw
    assert ATT_HEADS * ATT_HEAD_DIM == bw and X_HEADS * X_HEAD_DIM == bw

    n_gate = N_BRANCH * d
    col_dn = n_gate // LANES
    col_aq = col_dn + (dn_cols + bw) // LANES
    col_xq = col_aq + bw // LANES
    col_ak = col_xq + bw // LANES
    col_av = col_ak + 1
    n_proj = (col_av + 1) * LANES
    o_qkv, o_z, o_b, o_a = 0, dn_cols, dn_cols + bw, dn_cols + bw + 2 * DN_HEADS
    o_aq = o_a + 2 * DN_HEADS
    o_ak = o_aq + bw
    o_av = o_ak + ATT_KV_HEADS * ATT_HEAD_DIM
    o_xq = o_av + ATT_KV_HEADS * ATT_HEAD_DIM
    o_g = o_xq + bw

    cos_t, sin_t = _rope_tables(seq)
    x2 = x.reshape(batch * seq, d)
    mem2 = mem.reshape(batch * m_len, d)
    tile = lambda w: jnp.tile(w, LANES // ATT_HEAD_DIM)[None, :]

    for l in range(depth):
        wl = w_in[l].astype(BF16)
        w_prep = jnp.concatenate(
            [wl[:, o_g:], wl[:, o_qkv:o_b], wl[:, o_aq:o_ak], wl[:, o_xq:o_g], wl[:, o_ak:o_xq],
             _gate_lanes(wl[:, o_b:o_a], wl[:, o_a:o_aq])], axis=1)
        zeros8 = jnp.zeros((2 * DN_HEADS,), F32)
        gate_rows = lambda p: jnp.broadcast_to(
            _gate_lanes(zeros8, p.reshape(-1)).reshape(DN_GROUPS, 1, LANES)[:, :, :DN_GATE_ROWS, None],
            (DN_GROUPS, 1, DN_GATE_ROWS, LANES))
        gpar = jnp.concatenate([gate_rows(dn_a_log[l]), gate_rows(dn_dt_bias[l])], axis=1)

        proj, ba = _inproj(x2, mix_norm_w[l][None, :], w_prep, cos_t, sin_t, tile(q_norm_w[l]), tile(k_norm_w[l]),
                           n_proj, q_cols=(col_aq * LANES, col_xq * LANES), k_cols=(col_ak * LANES, col_av * LANES),
                           tm=512)
        ydn = _deltanet(proj, ba, dn_conv_w[l], gpar, dn_norm_w[l][None, :], batch, seq, col_dn)
        yatt = _attention(proj, batch, seq, col_aq, col_ak, col_av, tq=256)
        yx = _cross_attention(proj, mem2, mem_norm_w[l][None, :], w_mem_kv[l].astype(BF16),
                              batch, seq, m_len, col_xq, tm=512)
        x2 = _merge_ffn(x2, proj, ydn, yatt, yx, w_branch[l].astype(BF16), w_out[l].astype(BF16),
                        ffn_norm_w[l][None, :], w_up[l].astype(BF16), w_down[l].astype(BF16),
                        final_norm_w[None, :], final_norm=(l == depth - 1), tm=512)
    return x2.reshape(batch, seq, d)
```

```python
import functools

import numpy as np
import jax
import jax.numpy as jnp
from jax import lax
from jax.experimental import pallas as pl
from jax.experimental.pallas import tpu as pltpu

F32 = jnp.float32
BF16 = jnp.bfloat16

DN_HEADS = 4
DN_HEAD_DIM = 128
CONV_W = 5
ATT_HEADS = 8
ATT_KV_HEADS = 2
ATT_HEAD_DIM = 64
ROPE_PAIRS = ATT_HEAD_DIM // 4
ROPE_THETA = 10000.0
GRID_W = 64
X_HEADS = 4
X_HEAD_DIM = 128
N_BRANCH = 3
EPS = 1e-6
LOG2_E = 1.4426950408889634

LANES = 128
V7X_VMEM_LIMIT = 56 * 1024 * 1024

DN_CHUNK = 64
DN_BLOCK = 256
DN_HEADS_PER_STEP = 2
DN_GROUPS = DN_HEADS // DN_HEADS_PER_STEP
DN_ALPHA_LANE = 8
DN_GATE_ROWS = 2 * DN_ALPHA_LANE
DN_COL_BETA, DN_COL_GC, DN_COL_TOT, DN_COL_EG, DN_COL_EKD, DN_COL_ETOT = 0, 1, 3, 4, 6, 8

NT_DIMS = (((1,), (1,)), ((), ()))
TN_DIMS = (((0,), (0,)), ((), ()))
NEG_BIG = -1e30


def _rms(x, w):
    return (x * lax.rsqrt(jnp.mean(x * x, axis=-1, keepdims=True) + EPS)) * w


def _sigmoid(x):
    return 1.0 / (1.0 + jnp.exp(-x))


def _softplus(x):
    return jnp.maximum(x, 0.0) + jnp.log(1.0 + jnp.exp(-jnp.abs(x)))


def _params(semantics):
    return pltpu.CompilerParams(dimension_semantics=semantics, vmem_limit_bytes=V7X_VMEM_LIMIT)


def _head_norm_rope(x, w, cos, sin):
    lane = lax.broadcasted_iota(jnp.int32, (1, LANES), 1)
    low = lane < ATT_HEAD_DIM
    first = (lane & (2 * ROPE_PAIRS - 1)) < ROPE_PAIRS
    sq = x * x
    ms_lo = jnp.sum(jnp.where(low, sq, 0.0), axis=-1, keepdims=True)
    ms_hi = jnp.sum(jnp.where(low, 0.0, sq), axis=-1, keepdims=True)
    ms = jnp.where(low, ms_lo, ms_hi) * (1.0 / ATT_HEAD_DIM)
    y = (x * lax.rsqrt(ms + EPS)) * w
    rot = jnp.where(first, -pltpu.roll(y, LANES - ROPE_PAIRS, 1), pltpu.roll(y, ROPE_PAIRS, 1))
    return y * cos + rot * sin


def _inproj_kernel(x_ref, nw_ref, w_ref, cos_ref, sin_ref, qw_ref, kw_ref, proj_ref, ba_ref, *,
                   n_proj, col_chunk, q_cols, k_cols):
    h = _rms(x_ref[...], nw_ref[...]).astype(BF16)
    has_rope = lambda c0: any(max(c0, lo) < min(c0 + col_chunk, hi) for lo, hi in (q_cols, k_cols))
    for c0 in sorted(range(0, n_proj, col_chunk), key=lambda c0: not has_rope(c0)):
        c1 = min(c0 + col_chunk, n_proj)
        acc = lax.dot_general(h, w_ref[c0:c1, :], NT_DIMS, preferred_element_type=F32)
        for b0 in range(c0, c1, LANES):
            blk = acc[:, b0 - c0:b0 - c0 + LANES]
            if q_cols[0] <= b0 < q_cols[1]:
                blk = _head_norm_rope(blk, qw_ref[...], cos_ref[...], sin_ref[...]) * (ATT_HEAD_DIM ** -0.5 * LOG2_E)
            elif k_cols[0] <= b0 < k_cols[1]:
                blk = _head_norm_rope(blk, kw_ref[...], cos_ref[...], sin_ref[...])
            proj_ref[:, b0:b0 + LANES] = blk.astype(BF16)
    ba_ref[...] = lax.dot_general(h, w_ref[n_proj:, :], NT_DIMS, preferred_element_type=F32)


def _inproj(x2, norm_w, w_prep, cos_t, sin_t, qw, kw, n_proj, q_cols, k_cols, tm):
    t, d = x2.shape
    n_all = w_prep.shape[0]
    n_pos = cos_t.shape[0] // tm
    return pl.pallas_call(
        functools.partial(_inproj_kernel, n_proj=n_proj, col_chunk=512, q_cols=q_cols, k_cols=k_cols),
        grid=(t // tm,),
        in_specs=[
            pl.BlockSpec((tm, d), lambda i: (i, 0)),
            pl.BlockSpec((1, d), lambda i: (0, 0)),
            pl.BlockSpec((n_all, d), lambda i: (0, 0)),
            pl.BlockSpec((tm, LANES), lambda i: (i % n_pos, 0)),
            pl.BlockSpec((tm, LANES), lambda i: (i % n_pos, 0)),
            pl.BlockSpec((1, LANES), lambda i: (0, 0)),
            pl.BlockSpec((1, LANES), lambda i: (0, 0)),
        ],
        out_specs=[
            pl.BlockSpec((tm, n_proj), lambda i: (i, 0)),
            pl.BlockSpec((tm, n_all - n_proj), lambda i: (i, 0)),
        ],
        out_shape=[
            jax.ShapeDtypeStruct((t, n_proj), BF16),
            jax.ShapeDtypeStruct((t, n_all - n_proj), F32),
        ],
        compiler_params=_params(("arbitrary",)),
        name="inproj",
    )(x2, norm_w, w_prep, cos_t, sin_t, qw, kw)


def _deltanet_kernel(dq_ref, dk_ref, dv_ref, z_ref, ba_ref, wq_ref, wk_ref, wv_ref, gpar_ref, nw_ref,
                     y_ref, col_s, row_s, q_s, k_s, v_s, o_s, st_s, mask_s, bd_s, *pipe_s, seq, hp):
    c = DN_CHUNK
    blk = DN_BLOCK
    n_blk = seq // blk
    per_blk = blk // c
    hd = DN_HEAD_DIM

    gates = ba_ref[...].T[0:DN_GATE_ROWS, :]
    beta = _sigmoid(gates)
    g = -jnp.exp(gpar_ref[0][:, 0:1]) * _softplus(gates + gpar_ref[1][:, 0:1])
    lin = lax.broadcasted_iota(jnp.int32, (DN_GATE_ROWS, seq), 1) & (c - 1)
    pre = g
    suf = g
    sh = 1
    while sh < c:
        pre = pre + jnp.where(lin >= sh, pltpu.roll(pre, sh, 1), 0.0)
        suf = suf + jnp.where(lin < c - sh, pltpu.roll(suf, seq - sh, 1), 0.0)
        sh *= 2
    tot = pre + suf - g
    row_s[0] = pre
    row_s[1] = suf
    a0 = DN_ALPHA_LANE
    packed = [beta[0:a0], pre[a0:], suf[a0:], tot[a0:], jnp.exp(pre[a0:]), jnp.exp(suf[a0:]),
              jnp.exp(tot[a0:] - pre[a0:]), jnp.exp(tot[a0:] - suf[a0:]), jnp.exp(tot[a0:])]
    packed.append(jnp.zeros((LANES - a0 * len(packed), seq), F32))
    col_s[...] = jnp.concatenate(packed, axis=0).T

    ii = lax.broadcasted_iota(jnp.int32, (c, blk), 0)
    jj = lax.broadcasted_iota(jnp.int32, (c, blk), 1) & (c - 1)
    mask_s[0] = jnp.where(ii >= jj, 0.0, NEG_BIG)
    mask_s[1] = jnp.where(ii <= jj, 0.0, NEG_BIG)
    mask_s[2] = jnp.where(ii != jj, 1.0, 0.0)
    bi = lax.broadcasted_iota(jnp.int32, (blk, blk), 0) // c
    bj = lax.broadcasted_iota(jnp.int32, (blk, blk), 1) // c
    bd_s[...] = jnp.where(bi == bj, 1.0, 0.0).astype(BF16)
    lane_chunk = lax.broadcasted_iota(jnp.int32, (1, blk), 1) // c

    tpos = lax.broadcasted_iota(jnp.int32, (seq, hd), 0)

    def conv_silu(x_ref, w_ref, hh):
        x = x_ref[:, hh * hd:(hh + 1) * hd].astype(F32)
        w = w_ref[:, hh * hd:(hh + 1) * hd]
        y = x * w[2:3, :]
        y = y + jnp.where(tpos >= 2, pltpu.roll(x, 2, 0), 0.0) * w[0:1, :]
        y = y + jnp.where(tpos >= 1, pltpu.roll(x, 1, 0), 0.0) * w[1:2, :]
        y = y + jnp.where(tpos < seq - 1, pltpu.roll(x, seq - 1, 0), 0.0) * w[3:4, :]
        y = y + jnp.where(tpos < seq - 2, pltpu.roll(x, seq - 2, 0), 0.0) * w[4:5, :]
        return y * _sigmoid(y)

    def l2n(x):
        return x * lax.rsqrt(jnp.sum(x * x, axis=-1, keepdims=True) + EPS)

    for hh in range(hp):
        q_s[hh] = l2n(conv_silu(dq_ref, wq_ref, hh)) * (hd ** -0.5)
        k_s[hh] = l2n(conv_silu(dk_ref, wk_ref, hh))
        v_s[hh] = conv_silu(dv_ref, wv_ref, hh)
    o_s[...] = jnp.zeros(o_s.shape, F32)
    st_s[...] = jnp.zeros(st_s.shape, F32)

    n_double = c.bit_length() - 2

    chains = [(hh, d) for hh in range(hp) for d in range(2)]

    def diag_wide(a):
        w = a[(per_blk - 1) * c:]
        for i in range(per_blk - 2, -1, -1):
            w = jnp.where(lane_chunk == i, a[i * c:(i + 1) * c], w)
        return w

    def to_blockdiag16(w):
        return jnp.concatenate([w.astype(BF16)] * per_blk, axis=0) * bd_s[...]

    def block_rows(r, d):
        return pl.multiple_of((r if d == 0 else n_blk - 1 - r) * blk, blk)

    def prescan_stages(r, slot):
        pu, pw, pq, pk, pa = slot
        ld = []

        def load():
            for hh, d in chains:
                r0 = block_rows(r, d)
                ci = d * hp + hh
                cols = col_s[pl.ds(r0, blk), :]
                col = lambda group: cols[:, DN_ALPHA_LANE * group + ci:DN_ALPHA_LANE * group + ci + 1]
                e = dict(d=d, q=q_s[hh, pl.ds(r0, blk), :], k=k_s[hh, pl.ds(r0, blk), :],
                         v=v_s[hh, pl.ds(r0, blk), :], bcol=col(DN_COL_BETA), gcol=col(DN_COL_GC + d),
                         eg=col(DN_COL_EG + d), ekd=col(DN_COL_EKD + d),
                         grow=row_s[d, DN_ALPHA_LANE + ci:DN_ALPHA_LANE + ci + 1, pl.ds(r0, blk)])
                e["dec"] = jnp.exp((diag_wide(e["gcol"]) - e["grow"]) + mask_s[d])
                e["kb"] = e["k"] * e["bcol"]
                e["k16"] = e["k"].astype(BF16)
                ld.append(e)

        def kk_mm():
            for e in ld:
                e["kk"] = lax.dot_general(e["kb"].astype(BF16), e["k16"], NT_DIMS, preferred_element_type=F32)

        def qk_mm():
            for e in ld:
                e["qk"] = lax.dot_general(e["q"].astype(BF16), e["k16"], NT_DIMS, preferred_element_type=F32)

        def masked():
            for i, e in enumerate(ld):
                e["l_w"] = diag_wide(e["kk"]) * e["dec"] * mask_s[2]
                pa[i] = (diag_wide(e["qk"]) * e["dec"]).astype(BF16)

        def square():
            for e in ld:
                e["x"] = -e["l_w"]
                e["m"] = jnp.dot(e["l_w"].astype(BF16), to_blockdiag16(e["l_w"]), preferred_element_type=F32)

        def double_mm(last):
            for e in ld:
                x16 = e["x"].astype(BF16)
                lhs = x16 if last else jnp.concatenate([x16, e["m"].astype(BF16)], axis=0)
                e["both"] = jnp.dot(lhs, to_blockdiag16(e["m"]), preferred_element_type=F32)

        def double_add(last):
            for e in ld:
                e["x"] = e["x"] + e["m"] + e["both"][:c]
                if not last:
                    e["m"] = e["both"][c:]

        def make_rhs():
            for e in ld:
                e["rhs"] = jnp.concatenate([e["v"] * e["bcol"], e["kb"] * e["eg"]], axis=1)

        def uw_mm():
            for e in ld:
                e["uw"] = e["rhs"] + jnp.dot(to_blockdiag16(e["x"]), e["rhs"].astype(BF16),
                                             preferred_element_type=F32)

        def store():
            for i, e in enumerate(ld):
                pu[i] = e["uw"][:, :hd]
                pw[i] = e["uw"][:, hd:].astype(BF16)
                pq[i] = (e["q"] * e["eg"]).astype(BF16)
                pk[i] = (e["k"] * e["ekd"]).astype(BF16)

        stages = [load, kk_mm, qk_mm, masked, square]
        for it in range(n_double):
            last = it == n_double - 1
            stages += [functools.partial(double_mm, last), functools.partial(double_add, last)]
        return stages + [make_rhs, uw_mm, store]

    def scan_stages(r, slot):
        pu, pw, pq, pk, pa = slot
        ld = []

        def load():
            for i, (hh, d) in enumerate(chains):
                r0 = block_rows(r, d)
                lane = DN_ALPHA_LANE * DN_COL_ETOT + d * hp + hh
                ld.append(dict(i=i, hh=hh, d=d, r0=r0, etot=col_s[pl.ds(r0, blk), :][:, lane:lane + 1],
                               state=st_s[i]))

        def ws_mm(step):
            for e in ld:
                ci = step if e["d"] == 0 else per_blk - 1 - step
                lo, hi = ci * c, (ci + 1) * c
                e["lo"], e["hi"] = lo, hi
                lhs = jnp.concatenate([pw[e["i"], lo:hi, :], pq[e["i"], lo:hi, :]], axis=0)
                e["ws"] = jnp.dot(lhs, e["state"].astype(BF16), preferred_element_type=F32)

        def av_mm(step):
            for e in ld:
                lo, hi = e["lo"], e["hi"]
                e["vn16"] = (pu[e["i"], lo:hi, :] - e["ws"][:c]).astype(BF16)
                e["av"] = jnp.dot(pa[e["i"]][:, lo:hi], e["vn16"], preferred_element_type=F32)

        def kv_mm(step):
            for e in ld:
                lo, hi = e["lo"], e["hi"]
                e["kv"] = lax.dot_general(pk[e["i"], lo:hi, :], e["vn16"], TN_DIMS, preferred_element_type=F32)

        def update(step):
            for e in ld:
                lo = e["lo"]
                e["state"] = e["state"] * e["etot"][lo:lo + 1, :] + e["kv"]
                rows = pl.ds(e["r0"] + lo, c)
                o_s[e["hh"], rows, :] = o_s[e["hh"], rows, :] + (e["ws"][c:] + e["av"])

        def store():
            for e in ld:
                st_s[e["i"]] = e["state"]

        stages = [load]
        for step in range(per_blk):
            stages += [functools.partial(f, step) for f in (ws_mm, av_mm, kv_mm, update)]
        return stages + [store]

    def run_interleaved(*stage_lists):
        for k in range(max(len(s) for s in stage_lists)):
            for s in stage_lists:
                if k < len(s):
                    s[k]()

    slot_a, slot_b = pipe_s[:5], pipe_s[5:]
    run_interleaved(prescan_stages(0, slot_a))

    def two_block_steps(j, carry):
        r = 2 * j
        run_interleaved(prescan_stages(r + 1, slot_b), scan_stages(r, slot_a))
        run_interleaved(prescan_stages(jnp.minimum(r + 2, n_blk - 1), slot_a), scan_stages(r + 1, slot_b))
        return carry

    lax.fori_loop(0, n_blk // 2, two_block_steps, 0)

    for hh in range(hp):
        zz = z_ref[:, hh * hd:(hh + 1) * hd].astype(F32)
        y_ref[:, hh * hd:(hh + 1) * hd] = (_rms(o_s[hh], nw_ref[...]) * (zz * _sigmoid(zz))).astype(BF16)


def _deltanet(proj, ba, conv_w, gpar, norm_w, batch, seq, col0):
    hp = DN_HEADS_PER_STEP
    wblk = hp * DN_HEAD_DIM
    base = col0 * LANES // wblk
    per = DN_HEADS // hp
    t = batch * seq
    n_chain = 2 * hp
    return pl.pallas_call(
        functools.partial(_deltanet_kernel, seq=seq, hp=hp),
        grid=(batch, per),
        in_specs=[
            pl.BlockSpec((seq, wblk), lambda b, g: (b, base + g)),
            pl.BlockSpec((seq, wblk), lambda b, g: (b, base + per + g)),
            pl.BlockSpec((seq, wblk), lambda b, g: (b, base + 2 * per + g)),
            pl.BlockSpec((seq, wblk), lambda b, g: (b, base + 3 * per + g)),
            pl.BlockSpec((seq, LANES), lambda b, g: (b, g)),
            pl.BlockSpec((CONV_W, wblk), lambda b, g: (0, g)),
            pl.BlockSpec((CONV_W, wblk), lambda b, g: (0, per + g)),
            pl.BlockSpec((CONV_W, wblk), lambda b, g: (0, 2 * per + g)),
            pl.BlockSpec((None, 2, DN_GATE_ROWS, LANES), lambda b, g: (g, 0, 0, 0)),
            pl.BlockSpec((1, DN_HEAD_DIM), lambda b, g: (0, 0)),
        ],
        out_specs=pl.BlockSpec((seq, wblk), lambda b, g: (b, g)),
        out_shape=jax.ShapeDtypeStruct((t, DN_HEADS * DN_HEAD_DIM), BF16),
        scratch_shapes=[
            pltpu.VMEM((seq, LANES), F32),
            pltpu.VMEM((2, DN_GATE_ROWS, seq), F32),
            pltpu.VMEM((hp, seq, DN_HEAD_DIM), F32),
            pltpu.VMEM((hp, seq, DN_HEAD_DIM), F32),
            pltpu.VMEM((hp, seq, DN_HEAD_DIM), F32),
            pltpu.VMEM((hp, seq, DN_HEAD_DIM), F32),
            pltpu.VMEM((n_chain, DN_HEAD_DIM, DN_HEAD_DIM), F32),
            pltpu.VMEM((3, DN_CHUNK, DN_BLOCK), F32),
            pltpu.VMEM((DN_BLOCK, DN_BLOCK), BF16),
        ] + 2 * [
            pltpu.VMEM((n_chain, DN_BLOCK, DN_HEAD_DIM), F32),
            pltpu.VMEM((n_chain, DN_BLOCK, DN_HEAD_DIM), BF16),
            pltpu.VMEM((n_chain, DN_BLOCK, DN_HEAD_DIM), BF16),
            pltpu.VMEM((n_chain, DN_BLOCK, DN_HEAD_DIM), BF16),
            pltpu.VMEM((n_chain, DN_CHUNK, DN_BLOCK), BF16),
        ],
        compiler_params=_params(("arbitrary", "arbitrary")),
        name="deltanet",
    )(proj, proj, proj, proj, ba, conv_w, conv_w, conv_w, gpar, norm_w)


def _attn_kernel(q_ref, k_ref, v_ref, o_ref, klo_s, khi_s, vlo_s, vhi_s):
    g = pl.program_id(1)
    lane = lax.broadcasted_iota(jnp.int32, (1, LANES), 1)
    low = lane < ATT_HEAD_DIM

    @pl.when(pl.program_id(2) == 0)
    def _():
        in_g = (lane >= ATT_HEAD_DIM).astype(jnp.int32) == g
        is0 = g == 0
        for src_ref, lo_s, hi_s, ones_lane in ((k_ref, klo_s, khi_s, None), (v_ref, vlo_s, vhi_s, True)):
            sel = jnp.where(in_g, src_ref[...].astype(F32), 0.0)
            oth = pltpu.roll(sel, ATT_HEAD_DIM, 1)
            lo = jnp.where(is0, sel, oth)
            hi = jnp.where(is0, oth, sel)
            if ones_lane:
                lo = jnp.where(lane == ATT_HEAD_DIM, 1.0, lo)
                hi = jnp.where(lane == 0, 1.0, hi)
            lo_s[...] = lo.astype(BF16)
            hi_s[...] = hi.astype(BF16)

    n_jb = q_ref.shape[1] // LANES
    qbs = [q_ref[:, jb * LANES:(jb + 1) * LANES] for jb in range(n_jb)]
    units = [(jb, k_s, v_s) for jb in range(n_jb) for k_s, v_s in ((klo_s, vlo_s), (khi_s, vhi_s))]
    scores = [lax.dot_general(qbs[jb], k_s[...], NT_DIMS, preferred_element_type=F32) for jb, k_s, _ in units]
    probs = [jnp.exp2(s - jnp.max(s, axis=-1, keepdims=True)).astype(BF16) for s in scores]
    outs = [jnp.dot(p, v_s[...], preferred_element_type=F32) for p, (_, _, v_s) in zip(probs, units)]
    for jb in range(n_jb):
        o_lo, o_hi = outs[2 * jb], outs[2 * jb + 1]
        o_lo = o_lo / o_lo[:, ATT_HEAD_DIM:ATT_HEAD_DIM + 1]
        o_hi = o_hi / o_hi[:, 0:1]
        o_ref[:, jb * LANES:(jb + 1) * LANES] = jnp.where(low, o_lo, o_hi).astype(BF16)


def _attention(proj, batch, seq, q_col0, k_col, v_col, tq):
    t = batch * seq
    q_w = ATT_HEADS * ATT_HEAD_DIM // ATT_KV_HEADS
    q_base = q_col0 * LANES // q_w
    n_q = seq // tq
    return pl.pallas_call(
        _attn_kernel,
        grid=(batch, ATT_KV_HEADS, n_q),
        in_specs=[
            pl.BlockSpec((tq, q_w), lambda b, g, i: (b * n_q + i, q_base + g)),
            pl.BlockSpec((seq, LANES), lambda b, g, i: (b, k_col)),
            pl.BlockSpec((seq, LANES), lambda b, g, i: (b, v_col)),
        ],
        out_specs=pl.BlockSpec((tq, q_w), lambda b, g, i: (b * n_q + i, g)),
        out_shape=jax.ShapeDtypeStruct((t, ATT_HEADS * ATT_HEAD_DIM), BF16),
        scratch_shapes=[pltpu.VMEM((seq, LANES), BF16) for _ in range(4)],
        compiler_params=_params(("arbitrary", "arbitrary", "arbitrary")),
        name="gqa",
    )(proj, proj, proj)


def _xattn_kernel(q_ref, mem_ref, mw_ref, wkv_ref, o_ref, mk_s, mv_s):
    xw = X_HEADS * X_HEAD_DIM

    @pl.when(pl.program_id(1) == 0)
    def _():
        hm = _rms(mem_ref[...], mw_ref[...]).astype(BF16)
        kv = jnp.dot(hm, wkv_ref[...], preferred_element_type=F32)
        mk_s[...] = kv[:, :xw].astype(BF16)
        mv_s[...] = kv[:, xw:].astype(BF16)

    heads = [slice(h * X_HEAD_DIM, (h + 1) * X_HEAD_DIM) for h in range(X_HEADS)]
    scores = [lax.dot_general(q_ref[:, cols], mk_s[:, cols], NT_DIMS, preferred_element_type=F32)
              * (X_HEAD_DIM ** -0.5 * LOG2_E) for cols in heads]
    probs = [jnp.exp2(s - jnp.max(s, axis=-1, keepdims=True)) for s in scores]
    dens = [jnp.sum(p, axis=-1, keepdims=True) for p in probs]
    outs = [jnp.dot(p.astype(BF16), mv_s[:, cols], preferred_element_type=F32) for p, cols in zip(probs, heads)]
    for o, den, cols in zip(outs, dens, heads):
        o_ref[:, cols] = (o / den).astype(BF16)


def _cross_attention(proj, mem2, mem_norm_w, w_kv, batch, seq, m_len, q_col0, tm):
    t = batch * seq
    d = mem2.shape[1]
    xw = X_HEADS * X_HEAD_DIM
    q_base = q_col0 * LANES // xw
    n_t = seq // tm
    return pl.pallas_call(
        _xattn_kernel,
        grid=(batch, n_t),
        in_specs=[
            pl.BlockSpec((tm, xw), lambda b, i: (b * n_t + i, q_base)),
            pl.BlockSpec((m_len, d), lambda b, i: (b, 0)),
            pl.BlockSpec((1, d), lambda b, i: (0, 0)),
            pl.BlockSpec((d, 2 * xw), lambda b, i: (0, 0)),
        ],
        out_specs=pl.BlockSpec((tm, xw), lambda b, i: (b * n_t + i, 0)),
        out_shape=jax.ShapeDtypeStruct((t, xw), BF16),
        scratch_shapes=[pltpu.VMEM((m_len, xw), BF16), pltpu.VMEM((m_len, xw), BF16)],
        compiler_params=_params(("arbitrary", "arbitrary")),
        name="xattn",
    )(proj, mem2, mem_norm_w, w_kv)


def _merge_ffn_kernel(x_ref, g_ref, ydn_ref, yatt_ref, yx_ref, wb_ref, wo_ref, nw_ref, wu_ref, wd_ref, fw_ref,
                      o_ref, *, ff_chunk, final_norm):
    d = x_ref.shape[1]
    merged = None
    for n, y_ref in enumerate((ydn_ref, yatt_ref, yx_ref)):
        yb = jnp.dot(y_ref[...], wb_ref[n], preferred_element_type=F32)
        term = _sigmoid(g_ref[:, n * d:(n + 1) * d].astype(F32)) * yb
        merged = term if merged is None else merged + term
    x = x_ref[...] + jnp.dot(merged.astype(BF16), wo_ref[...], preferred_element_type=F32)
    h = _rms(x, nw_ref[...]).astype(BF16)
    acc = x
    for c0 in range(0, wu_ref.shape[1], ff_chunk):
        u = jnp.dot(h, wu_ref[:, c0:c0 + ff_chunk], preferred_element_type=F32)
        a = jnp.square(jnp.maximum(u, 0.0)).astype(BF16)
        acc = acc + jnp.dot(a, wd_ref[c0:c0 + ff_chunk, :], preferred_element_type=F32)
    o_ref[...] = _rms(acc, fw_ref[...]) if final_norm else acc


def _merge_ffn(x2, proj, ydn, yatt, yx, w_branch, w_out, norm_w, w_up, w_down, final_w, final_norm, tm):
    t, d = x2.shape
    bw = ydn.shape[1]
    dff = w_up.shape[1]
    resident = dict(pipeline_mode=pl.Buffered(1))
    return pl.pallas_call(
        functools.partial(_merge_ffn_kernel, ff_chunk=1024, final_norm=final_norm),
        grid=(t // tm,),
        in_specs=[
            pl.BlockSpec((tm, d), lambda i: (i, 0)),
            pl.BlockSpec((tm, N_BRANCH * d), lambda i: (i, 0)),
            pl.BlockSpec((tm, bw), lambda i: (i, 0)),
            pl.BlockSpec((tm, bw), lambda i: (i, 0)),
            pl.BlockSpec((tm, bw), lambda i: (i, 0)),
            pl.BlockSpec((N_BRANCH, bw, d), lambda i: (0, 0, 0), **resident),
            pl.BlockSpec((d, d), lambda i: (0, 0), **resident),
            pl.BlockSpec((1, d), lambda i: (0, 0)),
            pl.BlockSpec((d, dff), lambda i: (0, 0), **resident),
            pl.BlockSpec((dff, d), lambda i: (0, 0), **resident),
            pl.BlockSpec((1, d), lambda i: (0, 0)),
        ],
        out_specs=pl.BlockSpec((tm, d), lambda i: (i, 0)),
        out_shape=jax.ShapeDtypeStruct((t, d), F32),
        compiler_params=_params(("arbitrary",)),
        name="merge_ffn",
    )(x2, proj, ydn, yatt, yx, w_branch, w_out, norm_w, w_up, w_down, final_w)


def _gate_lanes(beta_like, alpha_like):
    hp = DN_HEADS_PER_STEP
    lead = beta_like.shape[:-1]
    blocks = []
    for g in range(DN_GROUPS):
        pick = lambda a: [a[..., dd * DN_HEADS + g * hp: dd * DN_HEADS + (g + 1) * hp] for dd in range(2)]
        blocks += pick(beta_like) + [jnp.zeros(lead + (DN_ALPHA_LANE - 2 * hp,), beta_like.dtype)]
        blocks += pick(alpha_like) + [jnp.zeros(lead + (LANES - DN_ALPHA_LANE - 2 * hp,), beta_like.dtype)]
    return jnp.concatenate(blocks, axis=-1)


def _rope_tables(seq):
    pos = np.arange(seq)
    inv_freq = (ROPE_THETA ** (-np.arange(ROPE_PAIRS, dtype=np.float32) / ROPE_PAIRS)).astype(np.float32)
    ang_r = (pos // GRID_W).astype(np.float32)[:, None] * inv_freq
    ang_c = (pos % GRID_W).astype(np.float32)[:, None] * inv_freq
    ang = np.concatenate([ang_r, ang_r, ang_c, ang_c] * (LANES // ATT_HEAD_DIM), axis=1)
    return jnp.asarray(np.cos(ang), F32), jnp.asarray(np.sin(ang), F32)


def kernel(x, mem, mix_norm_w, w_in, dn_conv_w, dn_a_log, dn_dt_bias, dn_norm_w, q_norm_w, k_norm_w,
           mem_norm_w, w_mem_kv, w_branch, w_out, ffn_norm_w, w_up, w_down, final_norm_w):
    batch, seq, d = x.shape
    m_len = mem.shape[1]
    depth = w_in.shape[0]
    bw = d // 2
    dn_cols = 3 * bw
    assert seq % DN_BLOCK == 0 and seq % 512 == 0 and DN_HEADS * DN_HEAD_DIM == bw
    assert ATT_HEADS * ATT_HEAD_DIM == bw and X_HEADS * X_HEAD_DIM == bw

    n_gate = N_BRANCH * d
    col_dn = n_gate // LANES
    col_aq = col_dn + (dn_cols + bw) // LANES
    col_xq = col_aq + bw // LANES
    col_ak = col_xq + bw // LANES
    col_av = col_ak + 1
    n_proj = (col_av + 1) * LANES
    o_qkv, o_z, o_b, o_a = 0, dn_cols, dn_cols + bw, dn_cols + bw + 2 * DN_HEADS
    o_aq = o_a + 2 * DN_HEADS
    o_ak = o_aq + bw
    o_av = o_ak + ATT_KV_HEADS * ATT_HEAD_DIM
    o_xq = o_av + ATT_KV_HEADS * ATT_HEAD_DIM
    o_g = o_xq + bw

    cos_t, sin_t = _rope_tables(seq)
    x2 = x.reshape(batch * seq, d)
    mem2 = mem.reshape(batch * m_len, d)
    tile = lambda w: jnp.tile(w, LANES // ATT_HEAD_DIM)[None, :]

    for l in range(depth):
        wt = jnp.swapaxes(w_in[l], 0, 1).astype(BF16)
        w_prep = jnp.concatenate(
            [wt[o_g:], wt[o_qkv:o_b], wt[o_aq:o_ak], wt[o_xq:o_g], wt[o_ak:o_xq],
             _gate_lanes(wt[o_b:o_a].T, wt[o_a:o_aq].T).T], axis=0)
        zeros8 = jnp.zeros((2 * DN_HEADS,), F32)
        gate_rows = lambda p: jnp.broadcast_to(
            _gate_lanes(zeros8, p.reshape(-1)).reshape(DN_GROUPS, 1, LANES)[:, :, :DN_GATE_ROWS, None],
            (DN_GROUPS, 1, DN_GATE_ROWS, LANES))
        gpar = jnp.concatenate([gate_rows(dn_a_log[l]), gate_rows(dn_dt_bias[l])], axis=1)

        proj, ba = _inproj(x2, mix_norm_w[l][None, :], w_prep, cos_t, sin_t, tile(q_norm_w[l]), tile(k_norm_w[l]),
                           n_proj, q_cols=(col_aq * LANES, col_xq * LANES), k_cols=(col_ak * LANES, col_av * LANES),
                           tm=512)
        ydn = _deltanet(proj, ba, dn_conv_w[l], gpar, dn_norm_w[l][None, :], batch, seq, col_dn)
        yatt = _attention(proj, batch, seq, col_aq, col_ak, col_av, tq=256)
        yx = _cross_attention(proj, mem2, mem_norm_w[l][None, :], w_mem_kv[l].astype(BF16),
                              batch, seq, m_len, col_xq, tm=seq)
        x2 = _merge_ffn(x2, proj, ydn, yatt, yx, w_branch[l].astype(BF16), w_out[l].astype(BF16),
                        ffn_norm_w[l][None, :], w_up[l].astype(BF16), w_down[l].astype(BF16),
                        final_norm_w[None, :], final_norm=(l == depth - 1), tm=512)
    return x2.reshape(batch, seq, d)
```

```python
import functools

import numpy as np
import jax
import jax.numpy as jnp
from jax import lax
from jax.experimental import pallas as pl
from jax.experimental.pallas import tpu as pltpu

F32 = jnp.float32
BF16 = jnp.bfloat16

DN_HEADS = 4
DN_HEAD_DIM = 128
CONV_W = 5
ATT_HEADS = 8
ATT_KV_HEADS = 2
ATT_HEAD_DIM = 64
ROPE_PAIRS = ATT_HEAD_DIM // 4
ROPE_THETA = 10000.0
GRID_W = 64
X_HEADS = 4
X_HEAD_DIM = 128
N_BRANCH = 3
EPS = 1e-6
LOG2_E = 1.4426950408889634

LANES = 128
V7X_VMEM_LIMIT = 56 * 1024 * 1024

DN_CHUNK = 64
DN_BLOCK = 256
DN_HEADS_PER_STEP = 4
DN_GROUPS = DN_HEADS // DN_HEADS_PER_STEP
DN_ALPHA_LANE = 8
DN_GATE_ROWS = 2 * DN_ALPHA_LANE
DN_COL_BETA, DN_COL_GC, DN_COL_TOT, DN_COL_EG, DN_COL_EKD, DN_COL_ETOT = 0, 1, 3, 4, 6, 8

NT_DIMS = (((1,), (1,)), ((), ()))
TN_DIMS = (((0,), (0,)), ((), ()))
NEG_BIG = -1e30


def _rms(x, w):
    return (x * lax.rsqrt(jnp.mean(x * x, axis=-1, keepdims=True) + EPS)) * w


def _sigmoid(x):
    return 1.0 / (1.0 + jnp.exp(-x))


def _softplus(x):
    return jnp.maximum(x, 0.0) + jnp.log(1.0 + jnp.exp(-jnp.abs(x)))


def _params(semantics):
    return pltpu.CompilerParams(dimension_semantics=semantics, vmem_limit_bytes=V7X_VMEM_LIMIT)


def _head_norm_rope(x, w, cos, sin):
    lane = lax.broadcasted_iota(jnp.int32, (1, LANES), 1)
    low = lane < ATT_HEAD_DIM
    first = (lane & (2 * ROPE_PAIRS - 1)) < ROPE_PAIRS
    sq = x * x
    ms_lo = jnp.sum(jnp.where(low, sq, 0.0), axis=-1, keepdims=True)
    ms_hi = jnp.sum(jnp.where(low, 0.0, sq), axis=-1, keepdims=True)
    ms = jnp.where(low, ms_lo, ms_hi) * (1.0 / ATT_HEAD_DIM)
    y = (x * lax.rsqrt(ms + EPS)) * w
    rot = jnp.where(first, -pltpu.roll(y, LANES - ROPE_PAIRS, 1), pltpu.roll(y, ROPE_PAIRS, 1))
    return y * cos + rot * sin


def _inproj_kernel(x_ref, nw_ref, w_ref, cos_ref, sin_ref, qw_ref, kw_ref, proj_ref, ba_ref, *,
                   n_proj, col_chunk, q_cols, k_cols):
    h = _rms(x_ref[...], nw_ref[...]).astype(BF16)
    has_rope = lambda c0: any(max(c0, lo) < min(c0 + col_chunk, hi) for lo, hi in (q_cols, k_cols))
    for c0 in sorted(range(0, n_proj, col_chunk), key=lambda c0: not has_rope(c0)):
        c1 = min(c0 + col_chunk, n_proj)
        acc = lax.dot_general(h, w_ref[c0:c1, :], NT_DIMS, preferred_element_type=F32)
        for b0 in range(c0, c1, LANES):
            blk = acc[:, b0 - c0:b0 - c0 + LANES]
            if q_cols[0] <= b0 < q_cols[1]:
                blk = _head_norm_rope(blk, qw_ref[...], cos_ref[...], sin_ref[...]) * (ATT_HEAD_DIM ** -0.5 * LOG2_E)
            elif k_cols[0] <= b0 < k_cols[1]:
                blk = _head_norm_rope(blk, kw_ref[...], cos_ref[...], sin_ref[...])
            proj_ref[:, b0:b0 + LANES] = blk.astype(BF16)
    ba_ref[...] = lax.dot_general(h, w_ref[n_proj:, :], NT_DIMS, preferred_element_type=F32)


def _inproj(x2, norm_w, w_prep, cos_t, sin_t, qw, kw, n_proj, q_cols, k_cols, tm):
    t, d = x2.shape
    n_all = w_prep.shape[0]
    n_pos = cos_t.shape[0] // tm
    return pl.pallas_call(
        functools.partial(_inproj_kernel, n_proj=n_proj, col_chunk=512, q_cols=q_cols, k_cols=k_cols),
        grid=(t // tm,),
        in_specs=[
            pl.BlockSpec((tm, d), lambda i: (i, 0)),
            pl.BlockSpec((1, d), lambda i: (0, 0)),
            pl.BlockSpec((n_all, d), lambda i: (0, 0)),
            pl.BlockSpec((tm, LANES), lambda i: (i % n_pos, 0)),
            pl.BlockSpec((tm, LANES), lambda i: (i % n_pos, 0)),
            pl.BlockSpec((1, LANES), lambda i: (0, 0)),
            pl.BlockSpec((1, LANES), lambda i: (0, 0)),
        ],
        out_specs=[
            pl.BlockSpec((tm, n_proj), lambda i: (i, 0)),
            pl.BlockSpec((tm, n_all - n_proj), lambda i: (i, 0)),
        ],
        out_shape=[
            jax.ShapeDtypeStruct((t, n_proj), BF16),
            jax.ShapeDtypeStruct((t, n_all - n_proj), F32),
        ],
        compiler_params=_params(("arbitrary",)),
        name="inproj",
    )(x2, norm_w, w_prep, cos_t, sin_t, qw, kw)


def _deltanet_kernel(dq_ref, dk_ref, dv_ref, z_ref, ba_ref, wq_ref, wk_ref, wv_ref, gpar_ref, nw_ref,
                     y_ref, col_s, row_s, q_s, k_s, v_s, o_s, st_s, mask_s, bd_s, *pipe_s, seq, hp):
    c = DN_CHUNK
    blk = DN_BLOCK
    n_blk = seq // blk
    per_blk = blk // c
    hd = DN_HEAD_DIM

    gates = ba_ref[...].T[0:DN_GATE_ROWS, :]
    beta = _sigmoid(gates)
    g = -jnp.exp(gpar_ref[0][:, 0:1]) * _softplus(gates + gpar_ref[1][:, 0:1])
    lin = lax.broadcasted_iota(jnp.int32, (DN_GATE_ROWS, seq), 1) & (c - 1)
    pre = g
    suf = g
    sh = 1
    while sh < c:
        pre = pre + jnp.where(lin >= sh, pltpu.roll(pre, sh, 1), 0.0)
        suf = suf + jnp.where(lin < c - sh, pltpu.roll(suf, seq - sh, 1), 0.0)
        sh *= 2
    tot = pre + suf - g
    row_s[0] = pre
    row_s[1] = suf
    a0 = DN_ALPHA_LANE
    packed = [beta[0:a0], pre[a0:], suf[a0:], tot[a0:], jnp.exp(pre[a0:]), jnp.exp(suf[a0:]),
              jnp.exp(tot[a0:] - pre[a0:]), jnp.exp(tot[a0:] - suf[a0:]), jnp.exp(tot[a0:])]
    packed.append(jnp.zeros((LANES - a0 * len(packed), seq), F32))
    col_s[...] = jnp.concatenate(packed, axis=0).T

    ii = lax.broadcasted_iota(jnp.int32, (c, blk), 0)
    jj = lax.broadcasted_iota(jnp.int32, (c, blk), 1) & (c - 1)
    mask_s[0] = jnp.where(ii >= jj, 0.0, NEG_BIG)
    mask_s[1] = jnp.where(ii <= jj, 0.0, NEG_BIG)
    mask_s[2] = jnp.where(ii != jj, 1.0, 0.0)
    bi = lax.broadcasted_iota(jnp.int32, (blk, blk), 0) // c
    bj = lax.broadcasted_iota(jnp.int32, (blk, blk), 1) // c
    bd_s[...] = jnp.where(bi == bj, 1.0, 0.0).astype(BF16)
    lane_chunk = lax.broadcasted_iota(jnp.int32, (1, blk), 1) // c

    tpos = lax.broadcasted_iota(jnp.int32, (seq, hd), 0)

    def conv_silu(x_ref, w_ref, hh):
        x = x_ref[:, hh * hd:(hh + 1) * hd].astype(F32)
        w = w_ref[:, hh * hd:(hh + 1) * hd]
        y = x * w[2:3, :]
        y = y + jnp.where(tpos >= 2, pltpu.roll(x, 2, 0), 0.0) * w[0:1, :]
        y = y + jnp.where(tpos >= 1, pltpu.roll(x, 1, 0), 0.0) * w[1:2, :]
        y = y + jnp.where(tpos < seq - 1, pltpu.roll(x, seq - 1, 0), 0.0) * w[3:4, :]
        y = y + jnp.where(tpos < seq - 2, pltpu.roll(x, seq - 2, 0), 0.0) * w[4:5, :]
        return y * _sigmoid(y)

    def l2n(x):
        return x * lax.rsqrt(jnp.sum(x * x, axis=-1, keepdims=True) + EPS)

    for hh in range(hp):
        q_s[hh] = l2n(conv_silu(dq_ref, wq_ref, hh)) * (hd ** -0.5)
        k_s[hh] = l2n(conv_silu(dk_ref, wk_ref, hh))
        v_s[hh] = conv_silu(dv_ref, wv_ref, hh)
    o_s[...] = jnp.zeros(o_s.shape, F32)
    st_s[...] = jnp.zeros(st_s.shape, F32)

    n_double = c.bit_length() - 2

    chains = [(hh, d) for hh in range(hp) for d in range(2)]

    def diag_wide(a):
        w = a[(per_blk - 1) * c:]
        for i in range(per_blk - 2, -1, -1):
            w = jnp.where(lane_chunk == i, a[i * c:(i + 1) * c], w)
        return w

    def to_blockdiag16(w):
        return jnp.concatenate([w.astype(BF16)] * per_blk, axis=0) * bd_s[...]

    def block_rows(r, d):
        return pl.multiple_of((r if d == 0 else n_blk - 1 - r) * blk, blk)

    def prescan_stages(r, slot):
        pu, pw, pq, pk, pa = slot
        ld = []

        def load():
            for hh, d in chains:
                r0 = block_rows(r, d)
                ci = d * hp + hh
                cols = col_s[pl.ds(r0, blk), :]
                col = lambda group: cols[:, DN_ALPHA_LANE * group + ci:DN_ALPHA_LANE * group + ci + 1]
                e = dict(d=d, q=q_s[hh, pl.ds(r0, blk), :], k=k_s[hh, pl.ds(r0, blk), :],
                         v=v_s[hh, pl.ds(r0, blk), :], bcol=col(DN_COL_BETA), gcol=col(DN_COL_GC + d),
                         eg=col(DN_COL_EG + d), ekd=col(DN_COL_EKD + d),
                         grow=row_s[d, DN_ALPHA_LANE + ci:DN_ALPHA_LANE + ci + 1, pl.ds(r0, blk)])
                e["dec"] = jnp.exp((diag_wide(e["gcol"]) - e["grow"]) + mask_s[d])
                e["kb"] = e["k"] * e["bcol"]
                e["k16"] = e["k"].astype(BF16)
                ld.append(e)

        def kk_mm():
            for e in ld:
                e["kk"] = lax.dot_general(e["kb"].astype(BF16), e["k16"], NT_DIMS, preferred_element_type=F32)

        def qk_mm():
            for e in ld:
                e["qk"] = lax.dot_general(e["q"].astype(BF16), e["k16"], NT_DIMS, preferred_element_type=F32)

        def masked():
            for i, e in enumerate(ld):
                e["l_w"] = diag_wide(e["kk"]) * e["dec"] * mask_s[2]
                pa[i] = (diag_wide(e["qk"]) * e["dec"]).astype(BF16)

        def square():
            for e in ld:
                e["x"] = -e["l_w"]
                e["m"] = jnp.dot(e["l_w"].astype(BF16), to_blockdiag16(e["l_w"]), preferred_element_type=F32)

        def double_mm(last):
            for e in ld:
                x16 = e["x"].astype(BF16)
                lhs = x16 if last else jnp.concatenate([x16, e["m"].astype(BF16)], axis=0)
                e["both"] = jnp.dot(lhs, to_blockdiag16(e["m"]), preferred_element_type=F32)

        def double_add(last):
            for e in ld:
                e["x"] = e["x"] + e["m"] + e["both"][:c]
                if not last:
                    e["m"] = e["both"][c:]

        def make_rhs():
            for e in ld:
                e["rhs"] = jnp.concatenate([e["v"] * e["bcol"], e["kb"] * e["eg"]], axis=1)

        def uw_mm():
            for e in ld:
                e["uw"] = e["rhs"] + jnp.dot(to_blockdiag16(e["x"]), e["rhs"].astype(BF16),
                                             preferred_element_type=F32)

        def store():
            for i, e in enumerate(ld):
                pu[i] = e["uw"][:, :hd]
                pw[i] = e["uw"][:, hd:].astype(BF16)
                pq[i] = (e["q"] * e["eg"]).astype(BF16)
                pk[i] = (e["k"] * e["ekd"]).astype(BF16)

        stages = [load, kk_mm, qk_mm, masked, square]
        for it in range(n_double):
            last = it == n_double - 1
            stages += [functools.partial(double_mm, last), functools.partial(double_add, last)]
        return stages + [make_rhs, uw_mm, store]

    def scan_stages(r, slot):
        pu, pw, pq, pk, pa = slot
        ld = []

        def load():
            for i, (hh, d) in enumerate(chains):
                r0 = block_rows(r, d)
                lane = DN_ALPHA_LANE * DN_COL_ETOT + d * hp + hh
                ld.append(dict(i=i, hh=hh, d=d, r0=r0, etot=col_s[pl.ds(r0, blk), :][:, lane:lane + 1],
                               state=st_s[i]))

        def ws_mm(step):
            for e in ld:
                ci = step if e["d"] == 0 else per_blk - 1 - step
                lo, hi = ci * c, (ci + 1) * c
                e["lo"], e["hi"] = lo, hi
                lhs = jnp.concatenate([pw[e["i"], lo:hi, :], pq[e["i"], lo:hi, :]], axis=0)
                e["ws"] = jnp.dot(lhs, e["state"].astype(BF16), preferred_element_type=F32)

        def av_mm(step):
            for e in ld:
                lo, hi = e["lo"], e["hi"]
                e["vn16"] = (pu[e["i"], lo:hi, :] - e["ws"][:c]).astype(BF16)
                e["av"] = jnp.dot(pa[e["i"]][:, lo:hi], e["vn16"], preferred_element_type=F32)

        def kv_mm(step):
            for e in ld:
                lo, hi = e["lo"], e["hi"]
                e["kv"] = lax.dot_general(pk[e["i"], lo:hi, :], e["vn16"], TN_DIMS, preferred_element_type=F32)

        def update(step):
            for e in ld:
                lo = e["lo"]
                e["state"] = e["state"] * e["etot"][lo:lo + 1, :] + e["kv"]
                rows = pl.ds(e["r0"] + lo, c)
                o_s[e["hh"], rows, :] = o_s[e["hh"], rows, :] + (e["ws"][c:] + e["av"])

        def store():
            for e in ld:
                st_s[e["i"]] = e["state"]

        stages = [load]
        for step in range(per_blk):
            stages += [functools.partial(f, step) for f in (ws_mm, av_mm, kv_mm, update)]
        return stages + [store]

    def run_interleaved(*stage_lists):
        for k in range(max(len(s) for s in stage_lists)):
            for s in stage_lists:
                if k < len(s):
                    s[k]()

    slot_a, slot_b = pipe_s[:5], pipe_s[5:]
    run_interleaved(prescan_stages(0, slot_a))

    def two_block_steps(j, carry):
        r = 2 * j
        run_interleaved(prescan_stages(r + 1, slot_b), scan_stages(r, slot_a))
        run_interleaved(prescan_stages(jnp.minimum(r + 2, n_blk - 1), slot_a), scan_stages(r + 1, slot_b))
        return carry

    lax.fori_loop(0, n_blk // 2, two_block_steps, 0)

    for hh in range(hp):
        zz = z_ref[:, hh * hd:(hh + 1) * hd].astype(F32)
        y_ref[:, hh * hd:(hh + 1) * hd] = (_rms(o_s[hh], nw_ref[...]) * (zz * _sigmoid(zz))).astype(BF16)


def _deltanet(proj, ba, conv_w, gpar, norm_w, batch, seq, col0):
    hp = DN_HEADS_PER_STEP
    wblk = hp * DN_HEAD_DIM
    base = col0 * LANES // wblk
    per = DN_HEADS // hp
    t = batch * seq
    n_chain = 2 * hp
    return pl.pallas_call(
        functools.partial(_deltanet_kernel, seq=seq, hp=hp),
        grid=(batch, per),
        in_specs=[
            pl.BlockSpec((seq, wblk), lambda b, g: (b, base + g)),
            pl.BlockSpec((seq, wblk), lambda b, g: (b, base + per + g)),
            pl.BlockSpec((seq, wblk), lambda b, g: (b, base + 2 * per + g)),
            pl.BlockSpec((seq, wblk), lambda b, g: (b, base + 3 * per + g), pipeline_mode=pl.Buffered(1)),
            pl.BlockSpec((seq, LANES), lambda b, g: (b, g), pipeline_mode=pl.Buffered(1)),
            pl.BlockSpec((CONV_W, wblk), lambda b, g: (0, g)),
            pl.BlockSpec((CONV_W, wblk), lambda b, g: (0, per + g)),
            pl.BlockSpec((CONV_W, wblk), lambda b, g: (0, 2 * per + g)),
            pl.BlockSpec((None, 2, DN_GATE_ROWS, LANES), lambda b, g: (g, 0, 0, 0)),
            pl.BlockSpec((1, DN_HEAD_DIM), lambda b, g: (0, 0)),
        ],
        out_specs=pl.BlockSpec((seq, wblk), lambda b, g: (b, g)),
        out_shape=jax.ShapeDtypeStruct((t, DN_HEADS * DN_HEAD_DIM), BF16),
        scratch_shapes=[
            pltpu.VMEM((seq, LANES), F32),
            pltpu.VMEM((2, DN_GATE_ROWS, seq), F32),
            pltpu.VMEM((hp, seq, DN_HEAD_DIM), F32),
            pltpu.VMEM((hp, seq, DN_HEAD_DIM), F32),
            pltpu.VMEM((hp, seq, DN_HEAD_DIM), F32),
            pltpu.VMEM((hp, seq, DN_HEAD_DIM), F32),
            pltpu.VMEM((n_chain, DN_HEAD_DIM, DN_HEAD_DIM), F32),
            pltpu.VMEM((3, DN_CHUNK, DN_BLOCK), F32),
            pltpu.VMEM((DN_BLOCK, DN_BLOCK), BF16),
        ] + 2 * [
            pltpu.VMEM((n_chain, DN_BLOCK, DN_HEAD_DIM), F32),
            pltpu.VMEM((n_chain, DN_BLOCK, DN_HEAD_DIM), BF16),
            pltpu.VMEM((n_chain, DN_BLOCK, DN_HEAD_DIM), BF16),
            pltpu.VMEM((n_chain, DN_BLOCK, DN_HEAD_DIM), BF16),
            pltpu.VMEM((n_chain, DN_CHUNK, DN_BLOCK), BF16),
        ],
        compiler_params=_params(("arbitrary", "arbitrary")),
        name="deltanet",
    )(proj, proj, proj, proj, ba, conv_w, conv_w, conv_w, gpar, norm_w)


def _attn_kernel(q_ref, k_ref, v_ref, o_ref, klo_s, khi_s, vlo_s, vhi_s):
    g = pl.program_id(1)
    lane = lax.broadcasted_iota(jnp.int32, (1, LANES), 1)
    low = lane < ATT_HEAD_DIM

    @pl.when(pl.program_id(2) == 0)
    def _():
        in_g = (lane >= ATT_HEAD_DIM).astype(jnp.int32) == g
        is0 = g == 0
        for src_ref, lo_s, hi_s, ones_lane in ((k_ref, klo_s, khi_s, None), (v_ref, vlo_s, vhi_s, True)):
            sel = jnp.where(in_g, src_ref[...].astype(F32), 0.0)
            oth = pltpu.roll(sel, ATT_HEAD_DIM, 1)
            lo = jnp.where(is0, sel, oth)
            hi = jnp.where(is0, oth, sel)
            if ones_lane:
                lo = jnp.where(lane == ATT_HEAD_DIM, 1.0, lo)
                hi = jnp.where(lane == 0, 1.0, hi)
            lo_s[...] = lo.astype(BF16)
            hi_s[...] = hi.astype(BF16)

    n_jb = q_ref.shape[1] // LANES
    qbs = [q_ref[:, jb * LANES:(jb + 1) * LANES] for jb in range(n_jb)]
    units = [(jb, k_s, v_s) for jb in range(n_jb) for k_s, v_s in ((klo_s, vlo_s), (khi_s, vhi_s))]
    scores = [lax.dot_general(qbs[jb], k_s[...], NT_DIMS, preferred_element_type=F32) for jb, k_s, _ in units]
    probs = [jnp.exp2(s - jnp.max(s, axis=-1, keepdims=True)).astype(BF16) for s in scores]
    outs = [jnp.dot(p, v_s[...], preferred_element_type=F32) for p, (_, _, v_s) in zip(probs, units)]
    for jb in range(n_jb):
        o_lo, o_hi = outs[2 * jb], outs[2 * jb + 1]
        o_lo = o_lo / o_lo[:, ATT_HEAD_DIM:ATT_HEAD_DIM + 1]
        o_hi = o_hi / o_hi[:, 0:1]
        o_ref[:, jb * LANES:(jb + 1) * LANES] = jnp.where(low, o_lo, o_hi).astype(BF16)


def _attention(proj, batch, seq, q_col0, k_col, v_col, tq):
    t = batch * seq
    q_w = ATT_HEADS * ATT_HEAD_DIM // ATT_KV_HEADS
    q_base = q_col0 * LANES // q_w
    n_q = seq // tq
    return pl.pallas_call(
        _attn_kernel,
        grid=(batch, ATT_KV_HEADS, n_q),
        in_specs=[
            pl.BlockSpec((tq, q_w), lambda b, g, i: (b * n_q + i, q_base + g)),
            pl.BlockSpec((seq, LANES), lambda b, g, i: (b, k_col)),
            pl.BlockSpec((seq, LANES), lambda b, g, i: (b, v_col)),
        ],
        out_specs=pl.BlockSpec((tq, q_w), lambda b, g, i: (b * n_q + i, g)),
        out_shape=jax.ShapeDtypeStruct((t, ATT_HEADS * ATT_HEAD_DIM), BF16),
        scratch_shapes=[pltpu.VMEM((seq, LANES), BF16) for _ in range(4)],
        compiler_params=_params(("arbitrary", "arbitrary", "arbitrary")),
        name="gqa",
    )(proj, proj, proj)


def _xattn_kernel(q_ref, mem_ref, mw_ref, wkv_ref, o_ref, mk_s, mv_s):
    xw = X_HEADS * X_HEAD_DIM

    @pl.when(pl.program_id(1) == 0)
    def _():
        hm = _rms(mem_ref[...], mw_ref[...]).astype(BF16)
        kv = jnp.dot(hm, wkv_ref[...], preferred_element_type=F32)
        mk_s[...] = kv[:, :xw].astype(BF16)
        mv_s[...] = kv[:, xw:].astype(BF16)

    heads = [slice(h * X_HEAD_DIM, (h + 1) * X_HEAD_DIM) for h in range(X_HEADS)]
    scores = [lax.dot_general(q_ref[:, cols], mk_s[:, cols], NT_DIMS, preferred_element_type=F32)
              * (X_HEAD_DIM ** -0.5 * LOG2_E) for cols in heads]
    probs = [jnp.exp2(s - jnp.max(s, axis=-1, keepdims=True)) for s in scores]
    dens = [jnp.sum(p, axis=-1, keepdims=True) for p in probs]
    outs = [jnp.dot(p.astype(BF16), mv_s[:, cols], preferred_element_type=F32) for p, cols in zip(probs, heads)]
    for o, den, cols in zip(outs, dens, heads):
        o_ref[:, cols] = (o / den).astype(BF16)


def _cross_attention(proj, mem2, mem_norm_w, w_kv, batch, seq, m_len, q_col0, tm):
    t = batch * seq
    d = mem2.shape[1]
    xw = X_HEADS * X_HEAD_DIM
    q_base = q_col0 * LANES // xw
    n_t = seq // tm
    return pl.pallas_call(
        _xattn_kernel,
        grid=(batch, n_t),
        in_specs=[
            pl.BlockSpec((tm, xw), lambda b, i: (b * n_t + i, q_base)),
            pl.BlockSpec((m_len, d), lambda b, i: (b, 0)),
            pl.BlockSpec((1, d), lambda b, i: (0, 0)),
            pl.BlockSpec((d, 2 * xw), lambda b, i: (0, 0)),
        ],
        out_specs=pl.BlockSpec((tm, xw), lambda b, i: (b * n_t + i, 0)),
        out_shape=jax.ShapeDtypeStruct((t, xw), BF16),
        scratch_shapes=[pltpu.VMEM((m_len, xw), BF16), pltpu.VMEM((m_len, xw), BF16)],
        compiler_params=_params(("arbitrary", "arbitrary")),
        name="xattn",
    )(proj, mem2, mem_norm_w, w_kv)


def _merge_ffn_kernel(x_ref, g_ref, ydn_ref, yatt_ref, yx_ref, wb_ref, wo_ref, nw_ref, wu_ref, wd_ref, fw_ref,
                      o_ref, *, ff_chunk, final_norm):
    d = x_ref.shape[1]
    merged = None
    for n, y_ref in enumerate((ydn_ref, yatt_ref, yx_ref)):
        yb = jnp.dot(y_ref[...], wb_ref[n], preferred_element_type=F32)
        term = _sigmoid(g_ref[:, n * d:(n + 1) * d].astype(F32)) * yb
        merged = term if merged is None else merged + term
    x = x_ref[...] + jnp.dot(merged.astype(BF16), wo_ref[...], preferred_element_type=F32)
    h = _rms(x, nw_ref[...]).astype(BF16)
    acc = x
    for c0 in range(0, wu_ref.shape[1], ff_chunk):
        u = jnp.dot(h, wu_ref[:, c0:c0 + ff_chunk], preferred_element_type=F32)
        a = jnp.square(jnp.maximum(u, 0.0)).astype(BF16)
        acc = acc + jnp.dot(a, wd_ref[c0:c0 + ff_chunk, :], preferred_element_type=F32)
    o_ref[...] = _rms(acc, fw_ref[...]) if final_norm else acc


def _merge_ffn(x2, proj, ydn, yatt, yx, w_branch, w_out, norm_w, w_up, w_down, final_w, final_norm, tm):
    t, d = x2.shape
    bw = ydn.shape[1]
    dff = w_up.shape[1]
    resident = dict(pipeline_mode=pl.Buffered(1))
    return pl.pallas_call(
        functools.partial(_merge_ffn_kernel, ff_chunk=1024, final_norm=final_norm),
        grid=(t // tm,),
        in_specs=[
            pl.BlockSpec((tm, d), lambda i: (i, 0)),
            pl.BlockSpec((tm, N_BRANCH * d), lambda i: (i, 0)),
            pl.BlockSpec((tm, bw), lambda i: (i, 0)),
            pl.BlockSpec((tm, bw), lambda i: (i, 0)),
            pl.BlockSpec((tm, bw), lambda i: (i, 0)),
            pl.BlockSpec((N_BRANCH, bw, d), lambda i: (0, 0, 0), **resident),
            pl.BlockSpec((d, d), lambda i: (0, 0), **resident),
            pl.BlockSpec((1, d), lambda i: (0, 0)),
            pl.BlockSpec((d, dff), lambda i: (0, 0), **resident),
            pl.BlockSpec((dff, d), lambda i: (0, 0), **resident),
            pl.BlockSpec((1, d), lambda i: (0, 0)),
        ],
        out_specs=pl.BlockSpec((tm, d), lambda i: (i, 0)),
        out_shape=jax.ShapeDtypeStruct((t, d), F32),
        compiler_params=_params(("arbitrary",)),
        name="merge_ffn",
    )(x2, proj, ydn, yatt, yx, w_branch, w_out, norm_w, w_up, w_down, final_w)


def _gate_lanes(beta_like, alpha_like):
    hp = DN_HEADS_PER_STEP
    lead = beta_like.shape[:-1]
    blocks = []
    for g in range(DN_GROUPS):
        pick = lambda a: [a[..., dd * DN_HEADS + g * hp: dd * DN_HEADS + (g + 1) * hp] for dd in range(2)]
        blocks += pick(beta_like) + [jnp.zeros(lead + (DN_ALPHA_LANE - 2 * hp,), beta_like.dtype)]
        blocks += pick(alpha_like) + [jnp.zeros(lead + (LANES - DN_ALPHA_LANE - 2 * hp,), beta_like.dtype)]
    return jnp.concatenate(blocks, axis=-1)


def _rope_tables(seq):
    pos = np.arange(seq)
    inv_freq = (ROPE_THETA ** (-np.arange(ROPE_PAIRS, dtype=np.float32) / ROPE_PAIRS)).astype(np.float32)
    ang_r = (pos // GRID_W).astype(np.float32)[:, None] * inv_freq
    ang_c = (pos % GRID_W).astype(np.float32)[:, None] * inv_freq
    ang = np.concatenate([ang_r, ang_r, ang_c, ang_c] * (LANES // ATT_HEAD_DIM), axis=1)
    return jnp.asarray(np.cos(ang), F32), jnp.asarray(np.sin(ang), F32)


def kernel(x, mem, mix_norm_w, w_in, dn_conv_w, dn_a_log, dn_dt_bias, dn_norm_w, q_norm_w, k_norm_w,
           mem_norm_w, w_mem_kv, w_branch, w_out, ffn_norm_w, w_up, w_down, final_norm_w):
    batch, seq, d = x.shape
    m_len = mem.shape[1]
    depth = w_in.shape[0]
    bw = d // 2
    dn_cols = 3 * bw
    assert seq % DN_BLOCK == 0 and seq % 512 == 0 and DN_HEADS * DN_HEAD_DIM == bw
    assert ATT_HEADS * ATT_HEAD_DIM == bw and X_HEADS * X_HEAD_DIM == bw

    n_gate = N_BRANCH * d
    col_dn = n_gate // LANES
    col_aq = col_dn + (dn_cols + bw) // LANES
    col_xq = col_aq + bw // LANES
    col_ak = col_xq + bw // LANES
    col_av = col_ak + 1
    n_proj = (col_av + 1) * LANES
    o_qkv, o_z, o_b, o_a = 0, dn_cols, dn_cols + bw, dn_cols + bw + 2 * DN_HEADS
    o_aq = o_a + 2 * DN_HEADS
    o_ak = o_aq + bw
    o_av = o_ak + ATT_KV_HEADS * ATT_HEAD_DIM
    o_xq = o_av + ATT_KV_HEADS * ATT_HEAD_DIM
    o_g = o_xq + bw

    cos_t, sin_t = _rope_tables(seq)
    x2 = x.reshape(batch * seq, d)
    mem2 = mem.reshape(batch * m_len, d)
    tile = lambda w: jnp.tile(w, LANES // ATT_HEAD_DIM)[None, :]

    for l in range(depth):
        wt = jnp.swapaxes(w_in[l], 0, 1).astype(BF16)
        w_prep = jnp.concatenate(
            [wt[o_g:], wt[o_qkv:o_b], wt[o_aq:o_ak], wt[o_xq:o_g], wt[o_ak:o_xq],
             _gate_lanes(wt[o_b:o_a].T, wt[o_a:o_aq].T).T], axis=0)
        zeros8 = jnp.zeros((2 * DN_HEADS,), F32)
        gate_rows = lambda p: jnp.broadcast_to(
            _gate_lanes(zeros8, p.reshape(-1)).reshape(DN_GROUPS, 1, LANES)[:, :, :DN_GATE_ROWS, None],
            (DN_GROUPS, 1, DN_GATE_ROWS, LANES))
        gpar = jnp.concatenate([gate_rows(dn_a_log[l]), gate_rows(dn_dt_bias[l])], axis=1)

        proj, ba = _inproj(x2, mix_norm_w[l][None, :], w_prep, cos_t, sin_t, tile(q_norm_w[l]), tile(k_norm_w[l]),
                           n_proj, q_cols=(col_aq * LANES, col_xq * LANES), k_cols=(col_ak * LANES, col_av * LANES),
                           tm=512)
        ydn = _deltanet(proj, ba, dn_conv_w[l], gpar, dn_norm_w[l][None, :], batch, seq, col_dn)
        yatt = _attention(proj, batch, seq, col_aq, col_ak, col_av, tq=512)
        yx = _cross_attention(proj, mem2, mem_norm_w[l][None, :], w_mem_kv[l].astype(BF16),
                              batch, seq, m_len, col_xq, tm=seq)
        x2 = _merge_ffn(x2, proj, ydn, yatt, yx, w_branch[l].astype(BF16), w_out[l].astype(BF16),
                        ffn_norm_w[l][None, :], w_up[l].astype(BF16), w_down[l].astype(BF16),
                        final_norm_w[None, :], final_norm=(l == depth - 1), tm=512)
    return x2.reshape(batch, seq, d)
```

```python
import functools

import numpy as np
import jax
import jax.numpy as jnp
from jax import lax
from jax.experimental import pallas as pl
from jax.experimental.pallas import tpu as pltpu

F32 = jnp.float32
BF16 = jnp.bfloat16

DN_HEADS = 4
DN_HEAD_DIM = 128
CONV_W = 5
ATT_HEADS = 8
ATT_KV_HEADS = 2
ATT_HEAD_DIM = 64
ROPE_PAIRS = ATT_HEAD_DIM // 4
ROPE_THETA = 10000.0
GRID_W = 64
X_HEADS = 4
X_HEAD_DIM = 128
N_BRANCH = 3
EPS = 1e-6
LOG2_E = 1.4426950408889634

LANES = 128
V7X_VMEM_LIMIT = 56 * 1024 * 1024

INPROJ_ROWS = 512
INPROJ_COL_CHUNK = 512
ATT_Q_ROWS = 512
MERGE_ROWS = 512
FFN_CHUNK = 1024

DN_CHUNK = 64
DN_BLOCK = 256
DN_HEADS_PER_STEP = 4
DN_GROUPS = DN_HEADS // DN_HEADS_PER_STEP
DN_ALPHA_LANE = 8
DN_GATE_ROWS = 2 * DN_ALPHA_LANE
DN_COL_BETA, DN_COL_GC, DN_COL_TOT, DN_COL_EG, DN_COL_EKD, DN_COL_ETOT = 0, 1, 3, 4, 6, 8

NT_DIMS = (((1,), (1,)), ((), ()))
TN_DIMS = (((0,), (0,)), ((), ()))
NEG_BIG = -1e30


def _rms(x, w):
    return (x * lax.rsqrt(jnp.mean(x * x, axis=-1, keepdims=True) + EPS)) * w


def _sigmoid(x):
    return 1.0 / (1.0 + jnp.exp(-x))


def _softplus(x):
    return jnp.maximum(x, 0.0) + jnp.log(1.0 + jnp.exp(-jnp.abs(x)))


def _params(semantics):
    return pltpu.CompilerParams(dimension_semantics=semantics, vmem_limit_bytes=V7X_VMEM_LIMIT)


def _head_norm_rope(x, w, cos, sin):
    lane = lax.broadcasted_iota(jnp.int32, (1, LANES), 1)
    low = lane < ATT_HEAD_DIM
    first = (lane & (2 * ROPE_PAIRS - 1)) < ROPE_PAIRS
    sq = x * x
    ms_lo = jnp.sum(jnp.where(low, sq, 0.0), axis=-1, keepdims=True)
    ms_hi = jnp.sum(jnp.where(low, 0.0, sq), axis=-1, keepdims=True)
    ms = jnp.where(low, ms_lo, ms_hi) * (1.0 / ATT_HEAD_DIM)
    y = (x * lax.rsqrt(ms + EPS)) * w
    rot = jnp.where(first, -pltpu.roll(y, LANES - ROPE_PAIRS, 1), pltpu.roll(y, ROPE_PAIRS, 1))
    return y * cos + rot * sin


def _inproj_kernel(x_ref, nw_ref, w_ref, cos_ref, sin_ref, qw_ref, kw_ref, proj_ref, ba_ref, *,
                   n_proj, col_chunk, q_cols, k_cols):
    h = _rms(x_ref[...], nw_ref[...]).astype(BF16)
    has_rope = lambda c0: any(max(c0, lo) < min(c0 + col_chunk, hi) for lo, hi in (q_cols, k_cols))
    for c0 in sorted(range(0, n_proj, col_chunk), key=lambda c0: not has_rope(c0)):
        c1 = min(c0 + col_chunk, n_proj)
        acc = lax.dot_general(h, w_ref[c0:c1, :], NT_DIMS, preferred_element_type=F32)
        for b0 in range(c0, c1, LANES):
            blk = acc[:, b0 - c0:b0 - c0 + LANES]
            if q_cols[0] <= b0 < q_cols[1]:
                blk = _head_norm_rope(blk, qw_ref[...], cos_ref[...], sin_ref[...]) * (ATT_HEAD_DIM ** -0.5 * LOG2_E)
            elif k_cols[0] <= b0 < k_cols[1]:
                blk = _head_norm_rope(blk, kw_ref[...], cos_ref[...], sin_ref[...])
            proj_ref[:, b0:b0 + LANES] = blk.astype(BF16)
    ba_ref[...] = lax.dot_general(h, w_ref[n_proj:, :], NT_DIMS, preferred_element_type=F32)


def _inproj(x2, norm_w, w_prep, cos_t, sin_t, qw, kw, n_proj, q_cols, k_cols, tm):
    t, d = x2.shape
    n_all = w_prep.shape[0]
    n_pos = cos_t.shape[0] // tm
    return pl.pallas_call(
        functools.partial(_inproj_kernel, n_proj=n_proj, col_chunk=INPROJ_COL_CHUNK, q_cols=q_cols, k_cols=k_cols),
        grid=(t // tm,),
        in_specs=[
            pl.BlockSpec((tm, d), lambda i: (i, 0)),
            pl.BlockSpec((1, d), lambda i: (0, 0)),
            pl.BlockSpec((n_all, d), lambda i: (0, 0)),
            pl.BlockSpec((tm, LANES), lambda i: (i % n_pos, 0)),
            pl.BlockSpec((tm, LANES), lambda i: (i % n_pos, 0)),
            pl.BlockSpec((1, LANES), lambda i: (0, 0)),
            pl.BlockSpec((1, LANES), lambda i: (0, 0)),
        ],
        out_specs=[
            pl.BlockSpec((tm, n_proj), lambda i: (i, 0)),
            pl.BlockSpec((tm, n_all - n_proj), lambda i: (i, 0)),
        ],
        out_shape=[
            jax.ShapeDtypeStruct((t, n_proj), BF16),
            jax.ShapeDtypeStruct((t, n_all - n_proj), F32),
        ],
        compiler_params=_params(("arbitrary",)),
        name="inproj",
    )(x2, norm_w, w_prep, cos_t, sin_t, qw, kw)


def _deltanet_kernel(dq_ref, dk_ref, dv_ref, z_ref, ba_ref, wq_ref, wk_ref, wv_ref, gpar_ref, nw_ref,
                     y_ref, col_s, row_s, q_s, k_s, v_s, o_s, st_s, mask_s, bd_s, *pipe_s, seq, hp):
    c = DN_CHUNK
    blk = DN_BLOCK
    n_blk = seq // blk
    per_blk = blk // c
    hd = DN_HEAD_DIM

    gates = ba_ref[...].T[0:DN_GATE_ROWS, :]
    beta = _sigmoid(gates)
    g = -jnp.exp(gpar_ref[0][:, 0:1]) * _softplus(gates + gpar_ref[1][:, 0:1])
    lin = lax.broadcasted_iota(jnp.int32, (DN_GATE_ROWS, seq), 1) & (c - 1)
    pre = g
    suf = g
    sh = 1
    while sh < c:
        pre = pre + jnp.where(lin >= sh, pltpu.roll(pre, sh, 1), 0.0)
        suf = suf + jnp.where(lin < c - sh, pltpu.roll(suf, seq - sh, 1), 0.0)
        sh *= 2
    tot = pre + suf - g
    row_s[0] = pre
    row_s[1] = suf
    a0 = DN_ALPHA_LANE
    packed = [beta[0:a0], pre[a0:], suf[a0:], tot[a0:], jnp.exp(pre[a0:]), jnp.exp(suf[a0:]),
              jnp.exp(tot[a0:] - pre[a0:]), jnp.exp(tot[a0:] - suf[a0:]), jnp.exp(tot[a0:])]
    packed.append(jnp.zeros((LANES - a0 * len(packed), seq), F32))
    col_s[...] = jnp.concatenate(packed, axis=0).T

    ii = lax.broadcasted_iota(jnp.int32, (c, blk), 0)
    jj = lax.broadcasted_iota(jnp.int32, (c, blk), 1) & (c - 1)
    mask_s[0] = jnp.where(ii >= jj, 0.0, NEG_BIG)
    mask_s[1] = jnp.where(ii <= jj, 0.0, NEG_BIG)
    mask_s[2] = jnp.where(ii != jj, 1.0, 0.0)
    bi = lax.broadcasted_iota(jnp.int32, (blk, blk), 0) // c
    bj = lax.broadcasted_iota(jnp.int32, (blk, blk), 1) // c
    bd_s[...] = jnp.where(bi == bj, 1.0, 0.0).astype(BF16)
    lane_chunk = lax.broadcasted_iota(jnp.int32, (1, blk), 1) // c

    tpos = lax.broadcasted_iota(jnp.int32, (seq, hd), 0)

    def conv_silu(x_ref, w_ref, hh):
        x = x_ref[:, hh * hd:(hh + 1) * hd].astype(F32)
        w = w_ref[:, hh * hd:(hh + 1) * hd]
        y = x * w[2:3, :]
        y = y + jnp.where(tpos >= 2, pltpu.roll(x, 2, 0), 0.0) * w[0:1, :]
        y = y + jnp.where(tpos >= 1, pltpu.roll(x, 1, 0), 0.0) * w[1:2, :]
        y = y + jnp.where(tpos < seq - 1, pltpu.roll(x, seq - 1, 0), 0.0) * w[3:4, :]
        y = y + jnp.where(tpos < seq - 2, pltpu.roll(x, seq - 2, 0), 0.0) * w[4:5, :]
        return y * _sigmoid(y)

    def l2n(x):
        return x * lax.rsqrt(jnp.sum(x * x, axis=-1, keepdims=True) + EPS)

    for hh in range(hp):
        q_s[hh] = l2n(conv_silu(dq_ref, wq_ref, hh)) * (hd ** -0.5)
        k_s[hh] = l2n(conv_silu(dk_ref, wk_ref, hh))
        v_s[hh] = conv_silu(dv_ref, wv_ref, hh)
    o_s[...] = jnp.zeros(o_s.shape, F32)
    st_s[...] = jnp.zeros(st_s.shape, F32)

    n_double = c.bit_length() - 2

    chains = [(hh, d) for hh in range(hp) for d in range(2)]

    def diag_wide(a):
        w = a[(per_blk - 1) * c:]
        for i in range(per_blk - 2, -1, -1):
            w = jnp.where(lane_chunk == i, a[i * c:(i + 1) * c], w)
        return w

    def to_blockdiag16(w):
        return jnp.concatenate([w.astype(BF16)] * per_blk, axis=0) * bd_s[...]

    def block_rows(r, d):
        return pl.multiple_of((r if d == 0 else n_blk - 1 - r) * blk, blk)

    def prescan_stages(r, slot):
        pu, pw, pq, pk, pa = slot
        ld = []

        def load():
            for hh, d in chains:
                r0 = block_rows(r, d)
                ci = d * hp + hh
                cols = col_s[pl.ds(r0, blk), :]
                col = lambda group: cols[:, DN_ALPHA_LANE * group + ci:DN_ALPHA_LANE * group + ci + 1]
                e = dict(d=d, q=q_s[hh, pl.ds(r0, blk), :], k=k_s[hh, pl.ds(r0, blk), :],
                         v=v_s[hh, pl.ds(r0, blk), :], bcol=col(DN_COL_BETA), gcol=col(DN_COL_GC + d),
                         eg=col(DN_COL_EG + d), ekd=col(DN_COL_EKD + d),
                         grow=row_s[d, DN_ALPHA_LANE + ci:DN_ALPHA_LANE + ci + 1, pl.ds(r0, blk)])
                e["dec"] = jnp.exp((diag_wide(e["gcol"]) - e["grow"]) + mask_s[d])
                e["kb"] = e["k"] * e["bcol"]
                e["k16"] = e["k"].astype(BF16)
                ld.append(e)

        def kk_mm():
            for e in ld:
                e["kk"] = lax.dot_general(e["kb"].astype(BF16), e["k16"], NT_DIMS, preferred_element_type=F32)

        def qk_mm():
            for e in ld:
                e["qk"] = lax.dot_general(e["q"].astype(BF16), e["k16"], NT_DIMS, preferred_element_type=F32)

        def masked():
            for i, e in enumerate(ld):
                e["l_w"] = diag_wide(e["kk"]) * e["dec"] * mask_s[2]
                pa[i] = (diag_wide(e["qk"]) * e["dec"]).astype(BF16)

        def square():
            for e in ld:
                e["x"] = -e["l_w"]
                e["m"] = jnp.dot(e["l_w"].astype(BF16), to_blockdiag16(e["l_w"]), preferred_element_type=F32)

        def double_mm(last):
            for e in ld:
                x16 = e["x"].astype(BF16)
                lhs = x16 if last else jnp.concatenate([x16, e["m"].astype(BF16)], axis=0)
                e["both"] = jnp.dot(lhs, to_blockdiag16(e["m"]), preferred_element_type=F32)

        def double_add(last):
            for e in ld:
                e["x"] = e["x"] + e["m"] + e["both"][:c]
                if not last:
                    e["m"] = e["both"][c:]

        def make_rhs():
            for e in ld:
                e["rhs"] = jnp.concatenate([e["v"] * e["bcol"], e["kb"] * e["eg"]], axis=1)

        def uw_mm():
            for e in ld:
                e["uw"] = e["rhs"] + jnp.dot(to_blockdiag16(e["x"]), e["rhs"].astype(BF16),
                                             preferred_element_type=F32)

        def store():
            for i, e in enumerate(ld):
                pu[i] = e["uw"][:, :hd]
                pw[i] = e["uw"][:, hd:].astype(BF16)
                pq[i] = (e["q"] * e["eg"]).astype(BF16)
                pk[i] = (e["k"] * e["ekd"]).astype(BF16)

        stages = [load, kk_mm, qk_mm, masked, square]
        for it in range(n_double):
            last = it == n_double - 1
            stages += [functools.partial(double_mm, last), functools.partial(double_add, last)]
        return stages + [make_rhs, uw_mm, store]

    def scan_stages(r, slot):
        pu, pw, pq, pk, pa = slot
        ld = []

        def load():
            for i, (hh, d) in enumerate(chains):
                r0 = block_rows(r, d)
                lane = DN_ALPHA_LANE * DN_COL_ETOT + d * hp + hh
                ld.append(dict(i=i, hh=hh, d=d, r0=r0, etot=col_s[pl.ds(r0, blk), :][:, lane:lane + 1],
                               state=st_s[i]))

        def ws_mm(step):
            for e in ld:
                ci = step if e["d"] == 0 else per_blk - 1 - step
                lo, hi = ci * c, (ci + 1) * c
                e["lo"], e["hi"] = lo, hi
                lhs = jnp.concatenate([pw[e["i"], lo:hi, :], pq[e["i"], lo:hi, :]], axis=0)
                e["ws"] = jnp.dot(lhs, e["state"].astype(BF16), preferred_element_type=F32)

        def av_mm(step):
            for e in ld:
                lo, hi = e["lo"], e["hi"]
                e["vn16"] = (pu[e["i"], lo:hi, :] - e["ws"][:c]).astype(BF16)
                e["av"] = jnp.dot(pa[e["i"]][:, lo:hi], e["vn16"], preferred_element_type=F32)

        def kv_mm(step):
            for e in ld:
                lo, hi = e["lo"], e["hi"]
                e["kv"] = lax.dot_general(pk[e["i"], lo:hi, :], e["vn16"], TN_DIMS, preferred_element_type=F32)

        def update(step):
            for e in ld:
                lo = e["lo"]
                e["state"] = e["state"] * e["etot"][lo:lo + 1, :] + e["kv"]
                rows = pl.ds(e["r0"] + lo, c)
                o_s[e["hh"], rows, :] = o_s[e["hh"], rows, :] + (e["ws"][c:] + e["av"])

        def store():
            for e in ld:
                st_s[e["i"]] = e["state"]

        stages = [load]
        for step in range(per_blk):
            stages += [functools.partial(f, step) for f in (ws_mm, av_mm, kv_mm, update)]
        return stages + [store]

    def run_interleaved(*stage_lists):
        for k in range(max(len(s) for s in stage_lists)):
            for s in stage_lists:
                if k < len(s):
                    s[k]()

    slot_a, slot_b = pipe_s[:5], pipe_s[5:]
    run_interleaved(prescan_stages(0, slot_a))

    def two_block_steps(j, carry):
        r = 2 * j
        run_interleaved(prescan_stages(r + 1, slot_b), scan_stages(r, slot_a))
        run_interleaved(prescan_stages(jnp.minimum(r + 2, n_blk - 1), slot_a), scan_stages(r + 1, slot_b))
        return carry

    lax.fori_loop(0, n_blk // 2, two_block_steps, 0)

    for hh in range(hp):
        zz = z_ref[:, hh * hd:(hh + 1) * hd].astype(F32)
        y_ref[:, hh * hd:(hh + 1) * hd] = (_rms(o_s[hh], nw_ref[...]) * (zz * _sigmoid(zz))).astype(BF16)


def _deltanet(proj, ba, conv_w, gpar, norm_w, batch, seq, col0):
    hp = DN_HEADS_PER_STEP
    wblk = hp * DN_HEAD_DIM
    base = col0 * LANES // wblk
    per = DN_HEADS // hp
    t = batch * seq
    n_chain = 2 * hp
    return pl.pallas_call(
        functools.partial(_deltanet_kernel, seq=seq, hp=hp),
        grid=(batch, per),
        in_specs=[
            pl.BlockSpec((seq, wblk), lambda b, g: (b, base + g)),
            pl.BlockSpec((seq, wblk), lambda b, g: (b, base + per + g)),
            pl.BlockSpec((seq, wblk), lambda b, g: (b, base + 2 * per + g)),
            pl.BlockSpec((seq, wblk), lambda b, g: (b, base + 3 * per + g), pipeline_mode=pl.Buffered(1)),
            pl.BlockSpec((seq, LANES), lambda b, g: (b, g), pipeline_mode=pl.Buffered(1)),
            pl.BlockSpec((CONV_W, wblk), lambda b, g: (0, g)),
            pl.BlockSpec((CONV_W, wblk), lambda b, g: (0, per + g)),
            pl.BlockSpec((CONV_W, wblk), lambda b, g: (0, 2 * per + g)),
            pl.BlockSpec((None, 2, DN_GATE_ROWS, LANES), lambda b, g: (g, 0, 0, 0)),
            pl.BlockSpec((1, DN_HEAD_DIM), lambda b, g: (0, 0)),
        ],
        out_specs=pl.BlockSpec((seq, wblk), lambda b, g: (b, g)),
        out_shape=jax.ShapeDtypeStruct((t, DN_HEADS * DN_HEAD_DIM), BF16),
        scratch_shapes=[
            pltpu.VMEM((seq, LANES), F32),
            pltpu.VMEM((2, DN_GATE_ROWS, seq), F32),
            pltpu.VMEM((hp, seq, DN_HEAD_DIM), F32),
            pltpu.VMEM((hp, seq, DN_HEAD_DIM), F32),
            pltpu.VMEM((hp, seq, DN_HEAD_DIM), F32),
            pltpu.VMEM((hp, seq, DN_HEAD_DIM), F32),
            pltpu.VMEM((n_chain, DN_HEAD_DIM, DN_HEAD_DIM), F32),
            pltpu.VMEM((3, DN_CHUNK, DN_BLOCK), F32),
            pltpu.VMEM((DN_BLOCK, DN_BLOCK), BF16),
        ] + 2 * [
            pltpu.VMEM((n_chain, DN_BLOCK, DN_HEAD_DIM), F32),
            pltpu.VMEM((n_chain, DN_BLOCK, DN_HEAD_DIM), BF16),
            pltpu.VMEM((n_chain, DN_BLOCK, DN_HEAD_DIM), BF16),
            pltpu.VMEM((n_chain, DN_BLOCK, DN_HEAD_DIM), BF16),
            pltpu.VMEM((n_chain, DN_CHUNK, DN_BLOCK), BF16),
        ],
        compiler_params=_params(("arbitrary", "arbitrary")),
        name="deltanet",
    )(proj, proj, proj, proj, ba, conv_w, conv_w, conv_w, gpar, norm_w)


def _attn_kernel(q_ref, k_ref, v_ref, o_ref, klo_s, khi_s, vlo_s, vhi_s):
    g = pl.program_id(1)
    lane = lax.broadcasted_iota(jnp.int32, (1, LANES), 1)
    low = lane < ATT_HEAD_DIM

    @pl.when(pl.program_id(2) == 0)
    def _():
        in_g = (lane >= ATT_HEAD_DIM).astype(jnp.int32) == g
        is0 = g == 0
        for src_ref, lo_s, hi_s, ones_lane in ((k_ref, klo_s, khi_s, None), (v_ref, vlo_s, vhi_s, True)):
            sel = jnp.where(in_g, src_ref[...].astype(F32), 0.0)
            oth = pltpu.roll(sel, ATT_HEAD_DIM, 1)
            lo = jnp.where(is0, sel, oth)
            hi = jnp.where(is0, oth, sel)
            if ones_lane:
                lo = jnp.where(lane == ATT_HEAD_DIM, 1.0, lo)
                hi = jnp.where(lane == 0, 1.0, hi)
            lo_s[...] = lo.astype(BF16)
            hi_s[...] = hi.astype(BF16)

    n_jb = q_ref.shape[1] // LANES
    qbs = [q_ref[:, jb * LANES:(jb + 1) * LANES] for jb in range(n_jb)]
    units = [(jb, k_s, v_s) for jb in range(n_jb) for k_s, v_s in ((klo_s, vlo_s), (khi_s, vhi_s))]
    scores = [lax.dot_general(qbs[jb], k_s[...], NT_DIMS, preferred_element_type=F32) for jb, k_s, _ in units]
    probs = [jnp.exp2(s - jnp.max(s, axis=-1, keepdims=True)).astype(BF16) for s in scores]
    outs = [jnp.dot(p, v_s[...], preferred_element_type=F32) for p, (_, _, v_s) in zip(probs, units)]
    for jb in range(n_jb):
        o_lo, o_hi = outs[2 * jb], outs[2 * jb + 1]
        o_lo = o_lo / o_lo[:, ATT_HEAD_DIM:ATT_HEAD_DIM + 1]
        o_hi = o_hi / o_hi[:, 0:1]
        o_ref[:, jb * LANES:(jb + 1) * LANES] = jnp.where(low, o_lo, o_hi).astype(BF16)


def _attention(proj, batch, seq, q_col0, k_col, v_col, tq):
    t = batch * seq
    q_w = ATT_HEADS * ATT_HEAD_DIM // ATT_KV_HEADS
    q_base = q_col0 * LANES // q_w
    n_q = seq // tq
    return pl.pallas_call(
        _attn_kernel,
        grid=(batch, ATT_KV_HEADS, n_q),
        in_specs=[
            pl.BlockSpec((tq, q_w), lambda b, g, i: (b * n_q + i, q_base + g)),
            pl.BlockSpec((seq, LANES), lambda b, g, i: (b, k_col)),
            pl.BlockSpec((seq, LANES), lambda b, g, i: (b, v_col)),
        ],
        out_specs=pl.BlockSpec((tq, q_w), lambda b, g, i: (b * n_q + i, g)),
        out_shape=jax.ShapeDtypeStruct((t, ATT_HEADS * ATT_HEAD_DIM), BF16),
        scratch_shapes=[pltpu.VMEM((seq, LANES), BF16) for _ in range(4)],
        compiler_params=_params(("arbitrary", "arbitrary", "arbitrary")),
        name="gqa",
    )(proj, proj, proj)


def _xattn_kernel(q_ref, mem_ref, mw_ref, wkv_ref, o_ref, mk_s, mv_s):
    xw = X_HEADS * X_HEAD_DIM

    @pl.when(pl.program_id(1) == 0)
    def _():
        hm = _rms(mem_ref[...], mw_ref[...]).astype(BF16)
        kv = jnp.dot(hm, wkv_ref[...], preferred_element_type=F32)
        mk_s[...] = kv[:, :xw].astype(BF16)
        mv_s[...] = kv[:, xw:].astype(BF16)

    heads = [slice(h * X_HEAD_DIM, (h + 1) * X_HEAD_DIM) for h in range(X_HEADS)]
    scores = [lax.dot_general(q_ref[:, cols], mk_s[:, cols], NT_DIMS, preferred_element_type=F32)
              * (X_HEAD_DIM ** -0.5 * LOG2_E) for cols in heads]
    probs = [jnp.exp2(s - jnp.max(s, axis=-1, keepdims=True)) for s in scores]
    dens = [jnp.sum(p, axis=-1, keepdims=True) for p in probs]
    outs = [jnp.dot(p.astype(BF16), mv_s[:, cols], preferred_element_type=F32) for p, cols in zip(probs, heads)]
    for o, den, cols in zip(outs, dens, heads):
        o_ref[:, cols] = (o / den).astype(BF16)


def _cross_attention(proj, mem2, mem_norm_w, w_kv, batch, seq, m_len, q_col0, tm):
    t = batch * seq
    d = mem2.shape[1]
    xw = X_HEADS * X_HEAD_DIM
    q_base = q_col0 * LANES // xw
    n_t = seq // tm
    return pl.pallas_call(
        _xattn_kernel,
        grid=(batch, n_t),
        in_specs=[
            pl.BlockSpec((tm, xw), lambda b, i: (b * n_t + i, q_base)),
            pl.BlockSpec((m_len, d), lambda b, i: (b, 0)),
            pl.BlockSpec((1, d), lambda b, i: (0, 0)),
            pl.BlockSpec((d, 2 * xw), lambda b, i: (0, 0)),
        ],
        out_specs=pl.BlockSpec((tm, xw), lambda b, i: (b * n_t + i, 0)),
        out_shape=jax.ShapeDtypeStruct((t, xw), BF16),
        scratch_shapes=[pltpu.VMEM((m_len, xw), BF16), pltpu.VMEM((m_len, xw), BF16)],
        compiler_params=_params(("arbitrary", "arbitrary")),
        name="xattn",
    )(proj, mem2, mem_norm_w, w_kv)


def _merge_ffn_kernel(x_ref, g_ref, ydn_ref, yatt_ref, yx_ref, wb_ref, wo_ref, nw_ref, wu_ref, wd_ref, fw_ref,
                      o_ref, *, ff_chunk, final_norm):
    d = x_ref.shape[1]
    merged = None
    for n, y_ref in enumerate((ydn_ref, yatt_ref, yx_ref)):
        yb = jnp.dot(y_ref[...], wb_ref[n], preferred_element_type=F32)
        term = _sigmoid(g_ref[:, n * d:(n + 1) * d].astype(F32)) * yb
        merged = term if merged is None else merged + term
    x = x_ref[...] + jnp.dot(merged.astype(BF16), wo_ref[...], preferred_element_type=F32)
    h = _rms(x, nw_ref[...]).astype(BF16)
    acc = x
    for c0 in range(0, wu_ref.shape[1], ff_chunk):
        u = jnp.dot(h, wu_ref[:, c0:c0 + ff_chunk], preferred_element_type=F32)
        a = jnp.square(jnp.maximum(u, 0.0)).astype(BF16)
        acc = acc + jnp.dot(a, wd_ref[c0:c0 + ff_chunk, :], preferred_element_type=F32)
    o_ref[...] = _rms(acc, fw_ref[...]) if final_norm else acc


def _merge_ffn(x2, proj, ydn, yatt, yx, w_branch, w_out, norm_w, w_up, w_down, final_w, final_norm, tm):
    t, d = x2.shape
    bw = ydn.shape[1]
    dff = w_up.shape[1]
    resident = dict(pipeline_mode=pl.Buffered(1))
    return pl.pallas_call(
        functools.partial(_merge_ffn_kernel, ff_chunk=FFN_CHUNK, final_norm=final_norm),
        grid=(t // tm,),
        in_specs=[
            pl.BlockSpec((tm, d), lambda i: (i, 0)),
            pl.BlockSpec((tm, N_BRANCH * d), lambda i: (i, 0)),
            pl.BlockSpec((tm, bw), lambda i: (i, 0)),
            pl.BlockSpec((tm, bw), lambda i: (i, 0)),
            pl.BlockSpec((tm, bw), lambda i: (i, 0)),
            pl.BlockSpec((N_BRANCH, bw, d), lambda i: (0, 0, 0), **resident),
            pl.BlockSpec((d, d), lambda i: (0, 0), **resident),
            pl.BlockSpec((1, d), lambda i: (0, 0)),
            pl.BlockSpec((d, dff), lambda i: (0, 0), **resident),
            pl.BlockSpec((dff, d), lambda i: (0, 0), **resident),
            pl.BlockSpec((1, d), lambda i: (0, 0)),
        ],
        out_specs=pl.BlockSpec((tm, d), lambda i: (i, 0)),
        out_shape=jax.ShapeDtypeStruct((t, d), F32),
        compiler_params=_params(("arbitrary",)),
        name="merge_ffn",
    )(x2, proj, ydn, yatt, yx, w_branch, w_out, norm_w, w_up, w_down, final_w)


def _gate_lanes(beta_like, alpha_like):
    hp = DN_HEADS_PER_STEP
    lead = beta_like.shape[:-1]
    blocks = []
    for g in range(DN_GROUPS):
        pick = lambda a: [a[..., dd * DN_HEADS + g * hp: dd * DN_HEADS + (g + 1) * hp] for dd in range(2)]
        blocks += pick(beta_like) + [jnp.zeros(lead + (DN_ALPHA_LANE - 2 * hp,), beta_like.dtype)]
        blocks += pick(alpha_like) + [jnp.zeros(lead + (LANES - DN_ALPHA_LANE - 2 * hp,), beta_like.dtype)]
    return jnp.concatenate(blocks, axis=-1)


def _rope_tables(seq):
    pos = np.arange(seq)
    inv_freq = (ROPE_THETA ** (-np.arange(ROPE_PAIRS, dtype=np.float32) / ROPE_PAIRS)).astype(np.float32)
    ang_r = (pos // GRID_W).astype(np.float32)[:, None] * inv_freq
    ang_c = (pos % GRID_W).astype(np.float32)[:, None] * inv_freq
    ang = np.concatenate([ang_r, ang_r, ang_c, ang_c] * (LANES // ATT_HEAD_DIM), axis=1)
    return jnp.asarray(np.cos(ang), F32), jnp.asarray(np.sin(ang), F32)


def kernel(x, mem, mix_norm_w, w_in, dn_conv_w, dn_a_log, dn_dt_bias, dn_norm_w, q_norm_w, k_norm_w,
           mem_norm_w, w_mem_kv, w_branch, w_out, ffn_norm_w, w_up, w_down, final_norm_w):
    batch, seq, d = x.shape
    m_len = mem.shape[1]
    depth = w_in.shape[0]
    bw = d // 2
    dn_cols = 3 * bw
    assert seq % (2 * DN_BLOCK) == 0 and DN_HEADS * DN_HEAD_DIM == bw
    assert seq % INPROJ_ROWS == 0 and seq % ATT_Q_ROWS == 0 and seq % MERGE_ROWS == 0
    assert ATT_HEADS * ATT_HEAD_DIM == bw and X_HEADS * X_HEAD_DIM == bw

    n_gate = N_BRANCH * d
    col_dn = n_gate // LANES
    col_aq = col_dn + (dn_cols + bw) // LANES
    col_xq = col_aq + bw // LANES
    col_ak = col_xq + bw // LANES
    col_av = col_ak + 1
    n_proj = (col_av + 1) * LANES
    o_qkv, o_z, o_b, o_a = 0, dn_cols, dn_cols + bw, dn_cols + bw + 2 * DN_HEADS
    o_aq = o_a + 2 * DN_HEADS
    o_ak = o_aq + bw
    o_av = o_ak + ATT_KV_HEADS * ATT_HEAD_DIM
    o_xq = o_av + ATT_KV_HEADS * ATT_HEAD_DIM
    o_g = o_xq + bw

    cos_t, sin_t = _rope_tables(seq)
    x2 = x.reshape(batch * seq, d)
    mem2 = mem.reshape(batch * m_len, d)
    tile = lambda w: jnp.tile(w, LANES // ATT_HEAD_DIM)[None, :]

    for l in range(depth):
        wt = jnp.swapaxes(w_in[l], 0, 1).astype(BF16)
        w_prep = jnp.concatenate(
            [wt[o_g:], wt[o_qkv:o_b], wt[o_aq:o_ak], wt[o_xq:o_g], wt[o_ak:o_xq],
             _gate_lanes(wt[o_b:o_a].T, wt[o_a:o_aq].T).T], axis=0)
        zeros8 = jnp.zeros((2 * DN_HEADS,), F32)
        gate_rows = lambda p: jnp.broadcast_to(
            _gate_lanes(zeros8, p.reshape(-1)).reshape(DN_GROUPS, 1, LANES)[:, :, :DN_GATE_ROWS, None],
            (DN_GROUPS, 1, DN_GATE_ROWS, LANES))
        gpar = jnp.concatenate([gate_rows(dn_a_log[l]), gate_rows(dn_dt_bias[l])], axis=1)

        proj, ba = _inproj(x2, mix_norm_w[l][None, :], w_prep, cos_t, sin_t, tile(q_norm_w[l]), tile(k_norm_w[l]),
                           n_proj, q_cols=(col_aq * LANES, col_xq * LANES), k_cols=(col_ak * LANES, col_av * LANES),
                           tm=INPROJ_ROWS)
        ydn = _deltanet(proj, ba, dn_conv_w[l], gpar, dn_norm_w[l][None, :], batch, seq, col_dn)
        yatt = _attention(proj, batch, seq, col_aq, col_ak, col_av, tq=ATT_Q_ROWS)
        yx = _cross_attention(proj, mem2, mem_norm_w[l][None, :], w_mem_kv[l].astype(BF16),
                              batch, seq, m_len, col_xq, tm=seq)
        x2 = _merge_ffn(x2, proj, ydn, yatt, yx, w_branch[l].astype(BF16), w_out[l].astype(BF16),
                        ffn_norm_w[l][None, :], w_up[l].astype(BF16), w_down[l].astype(BF16),
                        final_norm_w[None, :], final_norm=(l == depth - 1), tm=MERGE_ROWS)
    return x2.reshape(batch, seq, d)
```

```python
import functools

import numpy as np
import jax
import jax.numpy as jnp
from jax import lax
from jax.experimental import pallas as pl
from jax.experimental.pallas import tpu as pltpu

F32 = jnp.float32
BF16 = jnp.bfloat16

DN_HEADS = 4
DN_HEAD_DIM = 128
CONV_W = 5
ATT_HEADS = 8
ATT_KV_HEADS = 2
ATT_HEAD_DIM = 64
ROPE_PAIRS = ATT_HEAD_DIM // 4
ROPE_THETA = 10000.0
GRID_W = 64
X_HEADS = 4
X_HEAD_DIM = 128
N_BRANCH = 3
EPS = 1e-6
LOG2_E = 1.4426950408889634

LANES = 128
V7X_VMEM_LIMIT = 58 * 1024 * 1024

INPROJ_ROWS = 512
INPROJ_COL_CHUNK = 512
ATT_Q_ROWS = 512
MERGE_ROWS = 512
FFN_CHUNK = 1024

DN_CHUNK = 64
DN_BLOCK = 256
DN_HEADS_PER_STEP = 4
DN_GROUPS = DN_HEADS // DN_HEADS_PER_STEP
DN_ALPHA_LANE = 8
DN_GATE_ROWS = 2 * DN_ALPHA_LANE
DN_COL_BETA, DN_COL_GC, DN_COL_TOT, DN_COL_EG, DN_COL_EKD, DN_COL_ETOT = 0, 1, 3, 4, 6, 8

NT_DIMS = (((1,), (1,)), ((), ()))
TN_DIMS = (((0,), (0,)), ((), ()))
NEG_BIG = -1e30


def _rms(x, w):
    return (x * lax.rsqrt(jnp.mean(x * x, axis=-1, keepdims=True) + EPS)) * w


def _sigmoid(x):
    return 1.0 / (1.0 + jnp.exp(-x))


def _softplus(x):
    return jnp.maximum(x, 0.0) + jnp.log(1.0 + jnp.exp(-jnp.abs(x)))


def _params(semantics):
    return pltpu.CompilerParams(dimension_semantics=semantics, vmem_limit_bytes=V7X_VMEM_LIMIT)


def _head_norm_rope(x, w, cos, sin):
    lane = lax.broadcasted_iota(jnp.int32, (1, LANES), 1)
    low = lane < ATT_HEAD_DIM
    first = (lane & (2 * ROPE_PAIRS - 1)) < ROPE_PAIRS
    sq = x * x
    ms_lo = jnp.sum(jnp.where(low, sq, 0.0), axis=-1, keepdims=True)
    ms_hi = jnp.sum(jnp.where(low, 0.0, sq), axis=-1, keepdims=True)
    ms = jnp.where(low, ms_lo, ms_hi) * (1.0 / ATT_HEAD_DIM)
    y = (x * lax.rsqrt(ms + EPS)) * w
    rot = jnp.where(first, -pltpu.roll(y, LANES - ROPE_PAIRS, 1), pltpu.roll(y, ROPE_PAIRS, 1))
    return y * cos + rot * sin


def _inproj_kernel(x_ref, nw_ref, w_ref, cos_ref, sin_ref, qw_ref, kw_ref, proj_ref, ba_ref, *,
                   n_proj, col_chunk, q_cols, k_cols):
    h = _rms(x_ref[...], nw_ref[...]).astype(BF16)
    has_rope = lambda c0: any(max(c0, lo) < min(c0 + col_chunk, hi) for lo, hi in (q_cols, k_cols))
    for c0 in sorted(range(0, n_proj, col_chunk), key=lambda c0: not has_rope(c0)):
        c1 = min(c0 + col_chunk, n_proj)
        acc = lax.dot_general(h, w_ref[c0:c1, :], NT_DIMS, preferred_element_type=F32)
        for b0 in range(c0, c1, LANES):
            blk = acc[:, b0 - c0:b0 - c0 + LANES]
            if q_cols[0] <= b0 < q_cols[1]:
                blk = _head_norm_rope(blk, qw_ref[...], cos_ref[...], sin_ref[...]) * (ATT_HEAD_DIM ** -0.5 * LOG2_E)
            elif k_cols[0] <= b0 < k_cols[1]:
                blk = _head_norm_rope(blk, kw_ref[...], cos_ref[...], sin_ref[...])
            proj_ref[:, b0:b0 + LANES] = blk.astype(BF16)
    ba_ref[...] = lax.dot_general(h, w_ref[n_proj:, :], NT_DIMS, preferred_element_type=F32)


def _inproj(x2, norm_w, w_prep, cos_t, sin_t, qw, kw, n_proj, q_cols, k_cols, tm):
    t, d = x2.shape
    n_all = w_prep.shape[0]
    n_pos = cos_t.shape[0] // tm
    return pl.pallas_call(
        functools.partial(_inproj_kernel, n_proj=n_proj, col_chunk=INPROJ_COL_CHUNK, q_cols=q_cols, k_cols=k_cols),
        grid=(t // tm,),
        in_specs=[
            pl.BlockSpec((tm, d), lambda i: (i, 0)),
            pl.BlockSpec((1, d), lambda i: (0, 0)),
            pl.BlockSpec((n_all, d), lambda i: (0, 0)),
            pl.BlockSpec((tm, LANES), lambda i: (i % n_pos, 0)),
            pl.BlockSpec((tm, LANES), lambda i: (i % n_pos, 0)),
            pl.BlockSpec((1, LANES), lambda i: (0, 0)),
            pl.BlockSpec((1, LANES), lambda i: (0, 0)),
        ],
        out_specs=[
            pl.BlockSpec((tm, n_proj), lambda i: (i, 0)),
            pl.BlockSpec((tm, n_all - n_proj), lambda i: (i, 0)),
        ],
        out_shape=[
            jax.ShapeDtypeStruct((t, n_proj), BF16),
            jax.ShapeDtypeStruct((t, n_all - n_proj), F32),
        ],
        compiler_params=_params(("arbitrary",)),
        name="inproj",
    )(x2, norm_w, w_prep, cos_t, sin_t, qw, kw)


def _deltanet_kernel(dq_ref, dk_ref, dv_ref, z_ref, ba_ref, wq_ref, wk_ref, wv_ref, gpar_ref, nw_ref,
                     y_ref, col_s, row_s, q_s, k_s, v_s, o_s, st_s, mask_s, bd_s, *pipe_s, seq, hp):
    c = DN_CHUNK
    blk = DN_BLOCK
    n_blk = seq // blk
    per_blk = blk // c
    hd = DN_HEAD_DIM

    gates = ba_ref[...].T[0:DN_GATE_ROWS, :]
    beta = _sigmoid(gates)
    g = -jnp.exp(gpar_ref[0][:, 0:1]) * _softplus(gates + gpar_ref[1][:, 0:1])
    lin = lax.broadcasted_iota(jnp.int32, (DN_GATE_ROWS, seq), 1) & (c - 1)
    pre = g
    suf = g
    sh = 1
    while sh < c:
        pre = pre + jnp.where(lin >= sh, pltpu.roll(pre, sh, 1), 0.0)
        suf = suf + jnp.where(lin < c - sh, pltpu.roll(suf, seq - sh, 1), 0.0)
        sh *= 2
    tot = pre + suf - g
    row_s[0] = pre
    row_s[1] = suf
    a0 = DN_ALPHA_LANE
    packed = [beta[0:a0], pre[a0:], suf[a0:], tot[a0:], jnp.exp(pre[a0:]), jnp.exp(suf[a0:]),
              jnp.exp(tot[a0:] - pre[a0:]), jnp.exp(tot[a0:] - suf[a0:]), jnp.exp(tot[a0:])]
    packed.append(jnp.zeros((LANES - a0 * len(packed), seq), F32))
    col_s[...] = jnp.concatenate(packed, axis=0).T

    ii = lax.broadcasted_iota(jnp.int32, (c, blk), 0)
    jj = lax.broadcasted_iota(jnp.int32, (c, blk), 1) & (c - 1)
    mask_s[0] = jnp.where(ii >= jj, 0.0, NEG_BIG)
    mask_s[1] = jnp.where(ii <= jj, 0.0, NEG_BIG)
    mask_s[2] = jnp.where(ii != jj, 1.0, 0.0)
    bi = lax.broadcasted_iota(jnp.int32, (blk, blk), 0) // c
    bj = lax.broadcasted_iota(jnp.int32, (blk, blk), 1) // c
    bd_s[...] = jnp.where(bi == bj, 1.0, 0.0).astype(BF16)
    lane_chunk = lax.broadcasted_iota(jnp.int32, (1, blk), 1) // c

    tpos = lax.broadcasted_iota(jnp.int32, (seq, hd), 0)

    def conv_silu(x_ref, w_ref, hh):
        x = x_ref[:, hh * hd:(hh + 1) * hd].astype(F32)
        w = w_ref[:, hh * hd:(hh + 1) * hd]
        y = x * w[2:3, :]
        y = y + jnp.where(tpos >= 2, pltpu.roll(x, 2, 0), 0.0) * w[0:1, :]
        y = y + jnp.where(tpos >= 1, pltpu.roll(x, 1, 0), 0.0) * w[1:2, :]
        y = y + jnp.where(tpos < seq - 1, pltpu.roll(x, seq - 1, 0), 0.0) * w[3:4, :]
        y = y + jnp.where(tpos < seq - 2, pltpu.roll(x, seq - 2, 0), 0.0) * w[4:5, :]
        return y * _sigmoid(y)

    def l2n(x):
        return x * lax.rsqrt(jnp.sum(x * x, axis=-1, keepdims=True) + EPS)

    for hh in range(hp):
        q_s[hh] = l2n(conv_silu(dq_ref, wq_ref, hh)) * (hd ** -0.5)
        k_s[hh] = l2n(conv_silu(dk_ref, wk_ref, hh))
        v_s[hh] = conv_silu(dv_ref, wv_ref, hh)
    o_s[...] = jnp.zeros(o_s.shape, F32)
    st_s[...] = jnp.zeros(st_s.shape, F32)

    n_double = c.bit_length() - 2

    chains = [(hh, d) for hh in range(hp) for d in range(2)]

    def diag_wide(a):
        w = a[(per_blk - 1) * c:]
        for i in range(per_blk - 2, -1, -1):
            w = jnp.where(lane_chunk == i, a[i * c:(i + 1) * c], w)
        return w

    def to_blockdiag16(w):
        return jnp.concatenate([w.astype(BF16)] * per_blk, axis=0) * bd_s[...]

    def block_rows(r, d):
        return pl.multiple_of((r if d == 0 else n_blk - 1 - r) * blk, blk)

    def prescan_stages(r, slot):
        pu, pw, pq, pk, pa = slot
        ld = []

        def load():
            for hh, d in chains:
                r0 = block_rows(r, d)
                ci = d * hp + hh
                cols = col_s[pl.ds(r0, blk), :]
                col = lambda group: cols[:, DN_ALPHA_LANE * group + ci:DN_ALPHA_LANE * group + ci + 1]
                e = dict(d=d, q=q_s[hh, pl.ds(r0, blk), :], k=k_s[hh, pl.ds(r0, blk), :],
                         v=v_s[hh, pl.ds(r0, blk), :], bcol=col(DN_COL_BETA), gcol=col(DN_COL_GC + d),
                         eg=col(DN_COL_EG + d), ekd=col(DN_COL_EKD + d),
                         grow=row_s[d, DN_ALPHA_LANE + ci:DN_ALPHA_LANE + ci + 1, pl.ds(r0, blk)])
                e["dec"] = jnp.exp((diag_wide(e["gcol"]) - e["grow"]) + mask_s[d])
                e["kb"] = e["k"] * e["bcol"]
                e["k16"] = e["k"].astype(BF16)
                ld.append(e)

        def kk_mm():
            for e in ld:
                e["kk"] = lax.dot_general(e["kb"].astype(BF16), e["k16"], NT_DIMS, preferred_element_type=F32)

        def qk_mm():
            for e in ld:
                e["qk"] = lax.dot_general(e["q"].astype(BF16), e["k16"], NT_DIMS, preferred_element_type=F32)

        def masked():
            for i, e in enumerate(ld):
                e["l_w"] = diag_wide(e["kk"]) * e["dec"] * mask_s[2]
                pa[i] = (diag_wide(e["qk"]) * e["dec"]).astype(BF16)

        def square():
            for e in ld:
                e["x"] = -e["l_w"]
                e["m"] = jnp.dot(e["l_w"].astype(BF16), to_blockdiag16(e["l_w"]), preferred_element_type=F32)

        def double_mm(last):
            for e in ld:
                x16 = e["x"].astype(BF16)
                lhs = x16 if last else jnp.concatenate([x16, e["m"].astype(BF16)], axis=0)
                e["both"] = jnp.dot(lhs, to_blockdiag16(e["m"]), preferred_element_type=F32)

        def double_add(last):
            for e in ld:
                e["x"] = e["x"] + e["m"] + e["both"][:c]
                if not last:
                    e["m"] = e["both"][c:]

        def make_rhs():
            for e in ld:
                e["rhs"] = jnp.concatenate([e["v"] * e["bcol"], e["kb"] * e["eg"]], axis=1)

        def uw_mm():
            for e in ld:
                e["uw"] = e["rhs"] + jnp.dot(to_blockdiag16(e["x"]), e["rhs"].astype(BF16),
                                             preferred_element_type=F32)

        def store():
            for i, e in enumerate(ld):
                pu[i] = e["uw"][:, :hd]
                pw[i] = e["uw"][:, hd:].astype(BF16)
                pq[i] = (e["q"] * e["eg"]).astype(BF16)
                pk[i] = (e["k"] * e["ekd"]).astype(BF16)

        stages = [load, kk_mm, qk_mm, masked, square]
        for it in range(n_double):
            last = it == n_double - 1
            stages += [functools.partial(double_mm, last), functools.partial(double_add, last)]
        return stages + [make_rhs, uw_mm, store]

    def scan_stages(r, slot):
        pu, pw, pq, pk, pa = slot
        ld = []

        def load():
            for i, (hh, d) in enumerate(chains):
                r0 = block_rows(r, d)
                lane = DN_ALPHA_LANE * DN_COL_ETOT + d * hp + hh
                ld.append(dict(i=i, hh=hh, d=d, r0=r0, etot=col_s[pl.ds(r0, blk), :][:, lane:lane + 1],
                               state=st_s[i]))

        def ws_mm(step):
            for e in ld:
                ci = step if e["d"] == 0 else per_blk - 1 - step
                lo, hi = ci * c, (ci + 1) * c
                e["lo"], e["hi"] = lo, hi
                lhs = jnp.concatenate([pw[e["i"], lo:hi, :], pq[e["i"], lo:hi, :]], axis=0)
                e["ws"] = jnp.dot(lhs, e["state"].astype(BF16), preferred_element_type=F32)

        def av_mm(step):
            for e in ld:
                lo, hi = e["lo"], e["hi"]
                e["vn16"] = (pu[e["i"], lo:hi, :] - e["ws"][:c]).astype(BF16)
                e["av"] = jnp.dot(pa[e["i"]][:, lo:hi], e["vn16"], preferred_element_type=F32)

        def kv_mm(step):
            for e in ld:
                lo, hi = e["lo"], e["hi"]
                e["kv"] = lax.dot_general(pk[e["i"], lo:hi, :], e["vn16"], TN_DIMS, preferred_element_type=F32)

        def update(step):
            for e in ld:
                lo = e["lo"]
                e["state"] = e["state"] * e["etot"][lo:lo + 1, :] + e["kv"]
                rows = pl.ds(e["r0"] + lo, c)
                o_s[e["hh"], rows, :] = o_s[e["hh"], rows, :] + (e["ws"][c:] + e["av"])

        def store():
            for e in ld:
                st_s[e["i"]] = e["state"]

        stages = [load]
        for step in range(per_blk):
            stages += [functools.partial(f, step) for f in (ws_mm, av_mm, kv_mm, update)]
        return stages + [store]

    def run_interleaved(*stage_lists):
        for k in range(max(len(s) for s in stage_lists)):
            for s in stage_lists:
                if k < len(s):
                    s[k]()

    slot_a, slot_b = pipe_s[:5], pipe_s[5:]
    run_interleaved(prescan_stages(0, slot_a))

    def two_block_steps(j, carry):
        r = 2 * j
        run_interleaved(prescan_stages(r + 1, slot_b), scan_stages(r, slot_a))
        run_interleaved(prescan_stages(jnp.minimum(r + 2, n_blk - 1), slot_a), scan_stages(r + 1, slot_b))
        return carry

    lax.fori_loop(0, n_blk // 2, two_block_steps, 0)

    for hh in range(hp):
        zz = z_ref[:, hh * hd:(hh + 1) * hd].astype(F32)
        y_ref[:, hh * hd:(hh + 1) * hd] = (_rms(o_s[hh], nw_ref[...]) * (zz * _sigmoid(zz))).astype(BF16)


def _deltanet(proj, ba, conv_w, gpar, norm_w, batch, seq, col0):
    hp = DN_HEADS_PER_STEP
    wblk = hp * DN_HEAD_DIM
    base = col0 * LANES // wblk
    per = DN_HEADS // hp
    t = batch * seq
    n_chain = 2 * hp
    return pl.pallas_call(
        functools.partial(_deltanet_kernel, seq=seq, hp=hp),
        grid=(batch, per),
        in_specs=[
            pl.BlockSpec((seq, wblk), lambda b, g: (b, base + g)),
            pl.BlockSpec((seq, wblk), lambda b, g: (b, base + per + g)),
            pl.BlockSpec((seq, wblk), lambda b, g: (b, base + 2 * per + g)),
            pl.BlockSpec((seq, wblk), lambda b, g: (b, base + 3 * per + g)),
            pl.BlockSpec((seq, LANES), lambda b, g: (b, g)),
            pl.BlockSpec((CONV_W, wblk), lambda b, g: (0, g)),
            pl.BlockSpec((CONV_W, wblk), lambda b, g: (0, per + g)),
            pl.BlockSpec((CONV_W, wblk), lambda b, g: (0, 2 * per + g)),
            pl.BlockSpec((None, 2, DN_GATE_ROWS, LANES), lambda b, g: (g, 0, 0, 0)),
            pl.BlockSpec((1, DN_HEAD_DIM), lambda b, g: (0, 0)),
        ],
        out_specs=pl.BlockSpec((seq, wblk), lambda b, g: (b, g)),
        out_shape=jax.ShapeDtypeStruct((t, DN_HEADS * DN_HEAD_DIM), BF16),
        scratch_shapes=[
            pltpu.VMEM((seq, LANES), F32),
            pltpu.VMEM((2, DN_GATE_ROWS, seq), F32),
            pltpu.VMEM((hp, seq, DN_HEAD_DIM), F32),
            pltpu.VMEM((hp, seq, DN_HEAD_DIM), F32),
            pltpu.VMEM((hp, seq, DN_HEAD_DIM), F32),
            pltpu.VMEM((hp, seq, DN_HEAD_DIM), F32),
            pltpu.VMEM((n_chain, DN_HEAD_DIM, DN_HEAD_DIM), F32),
            pltpu.VMEM((3, DN_CHUNK, DN_BLOCK), F32),
            pltpu.VMEM((DN_BLOCK, DN_BLOCK), BF16),
        ] + 2 * [
            pltpu.VMEM((n_chain, DN_BLOCK, DN_HEAD_DIM), F32),
            pltpu.VMEM((n_chain, DN_BLOCK, DN_HEAD_DIM), BF16),
            pltpu.VMEM((n_chain, DN_BLOCK, DN_HEAD_DIM), BF16),
            pltpu.VMEM((n_chain, DN_BLOCK, DN_HEAD_DIM), BF16),
            pltpu.VMEM((n_chain, DN_CHUNK, DN_BLOCK), BF16),
        ],
        compiler_params=_params(("arbitrary", "arbitrary")),
        name="deltanet",
    )(proj, proj, proj, proj, ba, conv_w, conv_w, conv_w, gpar, norm_w)


def _attn_kernel(q_ref, k_ref, v_ref, o_ref, klo_s, khi_s, vlo_s, vhi_s):
    g = pl.program_id(1)
    lane = lax.broadcasted_iota(jnp.int32, (1, LANES), 1)
    low = lane < ATT_HEAD_DIM

    @pl.when(pl.program_id(2) == 0)
    def _():
        in_g = (lane >= ATT_HEAD_DIM).astype(jnp.int32) == g
        is0 = g == 0
        for src_ref, lo_s, hi_s, ones_lane in ((k_ref, klo_s, khi_s, None), (v_ref, vlo_s, vhi_s, True)):
            sel = jnp.where(in_g, src_ref[...].astype(F32), 0.0)
            oth = pltpu.roll(sel, ATT_HEAD_DIM, 1)
            lo = jnp.where(is0, sel, oth)
            hi = jnp.where(is0, oth, sel)
            if ones_lane:
                lo = jnp.where(lane == ATT_HEAD_DIM, 1.0, lo)
                hi = jnp.where(lane == 0, 1.0, hi)
            lo_s[...] = lo.astype(BF16)
            hi_s[...] = hi.astype(BF16)

    n_jb = q_ref.shape[1] // LANES
    qbs = [q_ref[:, jb * LANES:(jb + 1) * LANES] for jb in range(n_jb)]
    units = [(jb, k_s, v_s) for jb in range(n_jb) for k_s, v_s in ((klo_s, vlo_s), (khi_s, vhi_s))]
    scores = [lax.dot_general(qbs[jb], k_s[...], NT_DIMS, preferred_element_type=F32) for jb, k_s, _ in units]
    probs = [jnp.exp2(s - jnp.max(s, axis=-1, keepdims=True)).astype(BF16) for s in scores]
    outs = [jnp.dot(p, v_s[...], preferred_element_type=F32) for p, (_, _, v_s) in zip(probs, units)]
    for jb in range(n_jb):
        o_lo, o_hi = outs[2 * jb], outs[2 * jb + 1]
        o_lo = o_lo / o_lo[:, ATT_HEAD_DIM:ATT_HEAD_DIM + 1]
        o_hi = o_hi / o_hi[:, 0:1]
        o_ref[:, jb * LANES:(jb + 1) * LANES] = jnp.where(low, o_lo, o_hi).astype(BF16)


def _attention(proj, batch, seq, q_col0, k_col, v_col, tq):
    t = batch * seq
    q_w = ATT_HEADS * ATT_HEAD_DIM // ATT_KV_HEADS
    q_base = q_col0 * LANES // q_w
    n_q = seq // tq
    return pl.pallas_call(
        _attn_kernel,
        grid=(batch, ATT_KV_HEADS, n_q),
        in_specs=[
            pl.BlockSpec((tq, q_w), lambda b, g, i: (b * n_q + i, q_base + g)),
            pl.BlockSpec((seq, LANES), lambda b, g, i: (b, k_col)),
            pl.BlockSpec((seq, LANES), lambda b, g, i: (b, v_col)),
        ],
        out_specs=pl.BlockSpec((tq, q_w), lambda b, g, i: (b * n_q + i, g)),
        out_shape=jax.ShapeDtypeStruct((t, ATT_HEADS * ATT_HEAD_DIM), BF16),
        scratch_shapes=[pltpu.VMEM((seq, LANES), BF16) for _ in range(4)],
        compiler_params=_params(("arbitrary", "arbitrary", "arbitrary")),
        name="gqa",
    )(proj, proj, proj)


def _xattn_kernel(q_ref, mem_ref, mw_ref, wkv_ref, o_ref, mk_s, mv_s):
    xw = X_HEADS * X_HEAD_DIM

    @pl.when(pl.program_id(1) == 0)
    def _():
        hm = _rms(mem_ref[...], mw_ref[...]).astype(BF16)
        kv = jnp.dot(hm, wkv_ref[...], preferred_element_type=F32)
        mk_s[...] = kv[:, :xw].astype(BF16)
        mv_s[...] = kv[:, xw:].astype(BF16)

    heads = [slice(h * X_HEAD_DIM, (h + 1) * X_HEAD_DIM) for h in range(X_HEADS)]
    scores = [lax.dot_general(q_ref[:, cols], mk_s[:, cols], NT_DIMS, preferred_element_type=F32)
              * (X_HEAD_DIM ** -0.5 * LOG2_E) for cols in heads]
    probs = [jnp.exp2(s - jnp.max(s, axis=-1, keepdims=True)) for s in scores]
    dens = [jnp.sum(p, axis=-1, keepdims=True) for p in probs]
    outs = [jnp.dot(p.astype(BF16), mv_s[:, cols], preferred_element_type=F32) for p, cols in zip(probs, heads)]
    for o, den, cols in zip(outs, dens, heads):
        o_ref[:, cols] = (o / den).astype(BF16)


def _cross_attention(proj, mem2, mem_norm_w, w_kv, batch, seq, m_len, q_col0, tm):
    t = batch * seq
    d = mem2.shape[1]
    xw = X_HEADS * X_HEAD_DIM
    q_base = q_col0 * LANES // xw
    n_t = seq // tm
    return pl.pallas_call(
        _xattn_kernel,
        grid=(batch, n_t),
        in_specs=[
            pl.BlockSpec((tm, xw), lambda b, i: (b * n_t + i, q_base)),
            pl.BlockSpec((m_len, d), lambda b, i: (b, 0)),
            pl.BlockSpec((1, d), lambda b, i: (0, 0)),
            pl.BlockSpec((d, 2 * xw), lambda b, i: (0, 0)),
        ],
        out_specs=pl.BlockSpec((tm, xw), lambda b, i: (b * n_t + i, 0)),
        out_shape=jax.ShapeDtypeStruct((t, xw), BF16),
        scratch_shapes=[pltpu.VMEM((m_len, xw), BF16), pltpu.VMEM((m_len, xw), BF16)],
        compiler_params=_params(("arbitrary", "arbitrary")),
        name="xattn",
    )(proj, mem2, mem_norm_w, w_kv)


def _merge_ffn_kernel(x_ref, g_ref, ydn_ref, yatt_ref, yx_ref, wb_ref, wo_ref, nw_ref, wu_ref, wd_ref, fw_ref,
                      o_ref, *, ff_chunk, final_norm):
    d = x_ref.shape[1]
    merged = None
    for n, y_ref in enumerate((ydn_ref, yatt_ref, yx_ref)):
        yb = jnp.dot(y_ref[...], wb_ref[n], preferred_element_type=F32)
        term = _sigmoid(g_ref[:, n * d:(n + 1) * d].astype(F32)) * yb
        merged = term if merged is None else merged + term
    x = x_ref[...] + jnp.dot(merged.astype(BF16), wo_ref[...], preferred_element_type=F32)
    h = _rms(x, nw_ref[...]).astype(BF16)
    acc = x
    for c0 in range(0, wu_ref.shape[1], ff_chunk):
        u = jnp.dot(h, wu_ref[:, c0:c0 + ff_chunk], preferred_element_type=F32)
        a = jnp.square(jnp.maximum(u, 0.0)).astype(BF16)
        acc = acc + jnp.dot(a, wd_ref[c0:c0 + ff_chunk, :], preferred_element_type=F32)
    o_ref[...] = _rms(acc, fw_ref[...]) if final_norm else acc


def _merge_ffn(x2, proj, ydn, yatt, yx, w_branch, w_out, norm_w, w_up, w_down, final_w, final_norm, tm):
    t, d = x2.shape
    bw = ydn.shape[1]
    dff = w_up.shape[1]
    resident = dict(pipeline_mode=pl.Buffered(1))
    return pl.pallas_call(
        functools.partial(_merge_ffn_kernel, ff_chunk=FFN_CHUNK, final_norm=final_norm),
        grid=(t // tm,),
        in_specs=[
            pl.BlockSpec((tm, d), lambda i: (i, 0)),
            pl.BlockSpec((tm, N_BRANCH * d), lambda i: (i, 0)),
            pl.BlockSpec((tm, bw), lambda i: (i, 0)),
            pl.BlockSpec((tm, bw), lambda i: (i, 0)),
            pl.BlockSpec((tm, bw), lambda i: (i, 0)),
            pl.BlockSpec((N_BRANCH, bw, d), lambda i: (0, 0, 0), **resident),
            pl.BlockSpec((d, d), lambda i: (0, 0), **resident),
            pl.BlockSpec((1, d), lambda i: (0, 0)),
            pl.BlockSpec((d, dff), lambda i: (0, 0), **resident),
            pl.BlockSpec((dff, d), lambda i: (0, 0), **resident),
            pl.BlockSpec((1, d), lambda i: (0, 0)),
        ],
        out_specs=pl.BlockSpec((tm, d), lambda i: (i, 0)),
        out_shape=jax.ShapeDtypeStruct((t, d), F32),
        compiler_params=_params(("arbitrary",)),
        name="merge_ffn",
    )(x2, proj, ydn, yatt, yx, w_branch, w_out, norm_w, w_up, w_down, final_w)


def _gate_lanes(beta_like, alpha_like):
    hp = DN_HEADS_PER_STEP
    lead = beta_like.shape[:-1]
    blocks = []
    for g in range(DN_GROUPS):
        pick = lambda a: [a[..., dd * DN_HEADS + g * hp: dd * DN_HEADS + (g + 1) * hp] for dd in range(2)]
        blocks += pick(beta_like) + [jnp.zeros(lead + (DN_ALPHA_LANE - 2 * hp,), beta_like.dtype)]
        blocks += pick(alpha_like) + [jnp.zeros(lead + (LANES - DN_ALPHA_LANE - 2 * hp,), beta_like.dtype)]
    return jnp.concatenate(blocks, axis=-1)


def _rope_tables(seq):
    pos = np.arange(seq)
    inv_freq = (ROPE_THETA ** (-np.arange(ROPE_PAIRS, dtype=np.float32) / ROPE_PAIRS)).astype(np.float32)
    ang_r = (pos // GRID_W).astype(np.float32)[:, None] * inv_freq
    ang_c = (pos % GRID_W).astype(np.float32)[:, None] * inv_freq
    ang = np.concatenate([ang_r, ang_r, ang_c, ang_c] * (LANES // ATT_HEAD_DIM), axis=1)
    return jnp.asarray(np.cos(ang), F32), jnp.asarray(np.sin(ang), F32)


def kernel(x, mem, mix_norm_w, w_in, dn_conv_w, dn_a_log, dn_dt_bias, dn_norm_w, q_norm_w, k_norm_w,
           mem_norm_w, w_mem_kv, w_branch, w_out, ffn_norm_w, w_up, w_down, final_norm_w):
    batch, seq, d = x.shape
    m_len = mem.shape[1]
    depth = w_in.shape[0]
    bw = d // 2
    dn_cols = 3 * bw
    assert seq % (2 * DN_BLOCK) == 0 and DN_HEADS * DN_HEAD_DIM == bw
    assert seq % INPROJ_ROWS == 0 and seq % ATT_Q_ROWS == 0 and seq % MERGE_ROWS == 0
    assert ATT_HEADS * ATT_HEAD_DIM == bw and X_HEADS * X_HEAD_DIM == bw

    n_gate = N_BRANCH * d
    col_dn = n_gate // LANES
    col_aq = col_dn + (dn_cols + bw) // LANES
    col_xq = col_aq + bw // LANES
    col_ak = col_xq + bw // LANES
    col_av = col_ak + 1
    n_proj = (col_av + 1) * LANES
    o_qkv, o_z, o_b, o_a = 0, dn_cols, dn_cols + bw, dn_cols + bw + 2 * DN_HEADS
    o_aq = o_a + 2 * DN_HEADS
    o_ak = o_aq + bw
    o_av = o_ak + ATT_KV_HEADS * ATT_HEAD_DIM
    o_xq = o_av + ATT_KV_HEADS * ATT_HEAD_DIM
    o_g = o_xq + bw

    cos_t, sin_t = _rope_tables(seq)
    x2 = x.reshape(batch * seq, d)
    mem2 = mem.reshape(batch * m_len, d)
    tile = lambda w: jnp.tile(w, LANES // ATT_HEAD_DIM)[None, :]

    for l in range(depth):
        wt = jnp.swapaxes(w_in[l], 0, 1).astype(BF16)
        w_prep = jnp.concatenate(
            [wt[o_g:], wt[o_qkv:o_b], wt[o_aq:o_ak], wt[o_xq:o_g], wt[o_ak:o_xq],
             _gate_lanes(wt[o_b:o_a].T, wt[o_a:o_aq].T).T], axis=0)
        zeros8 = jnp.zeros((2 * DN_HEADS,), F32)
        gate_rows = lambda p: jnp.broadcast_to(
            _gate_lanes(zeros8, p.reshape(-1)).reshape(DN_GROUPS, 1, LANES)[:, :, :DN_GATE_ROWS, None],
            (DN_GROUPS, 1, DN_GATE_ROWS, LANES))
        gpar = jnp.concatenate([gate_rows(dn_a_log[l]), gate_rows(dn_dt_bias[l])], axis=1)

        proj, ba = _inproj(x2, mix_norm_w[l][None, :], w_prep, cos_t, sin_t, tile(q_norm_w[l]), tile(k_norm_w[l]),
                           n_proj, q_cols=(col_aq * LANES, col_xq * LANES), k_cols=(col_ak * LANES, col_av * LANES),
                           tm=INPROJ_ROWS)
        ydn = _deltanet(proj, ba, dn_conv_w[l], gpar, dn_norm_w[l][None, :], batch, seq, col_dn)
        yatt = _attention(proj, batch, seq, col_aq, col_ak, col_av, tq=ATT_Q_ROWS)
        yx = _cross_attention(proj, mem2, mem_norm_w[l][None, :], w_mem_kv[l].astype(BF16),
                              batch, seq, m_len, col_xq, tm=seq)
        x2 = _merge_ffn(x2, proj, ydn, yatt, yx, w_branch[l].astype(BF16), w_out[l].astype(BF16),
                        ffn_norm_w[l][None, :], w_up[l].astype(BF16), w_down[l].astype(BF16),
                        final_norm_w[None, :], final_norm=(l == depth - 1), tm=MERGE_ROWS)
    return x2.reshape(batch, seq, d)
```

```python
import functools

import numpy as np
import jax
import jax.numpy as jnp
from jax import lax
from jax.experimental import pallas as pl
from jax.experimental.pallas import tpu as pltpu

F32 = jnp.float32
BF16 = jnp.bfloat16

DN_HEADS = 4
DN_HEAD_DIM = 128
CONV_W = 5
ATT_HEADS = 8
ATT_KV_HEADS = 2
ATT_HEAD_DIM = 64
ROPE_PAIRS = ATT_HEAD_DIM // 4
ROPE_THETA = 10000.0
GRID_W = 64
X_HEADS = 4
X_HEAD_DIM = 128
N_BRANCH = 3
EPS = 1e-6
LOG2_E = 1.4426950408889634

LANES = 128
V7X_VMEM_LIMIT = 58 * 1024 * 1024

INPROJ_ROWS = 512
INPROJ_COL_CHUNK = 512
ATT_Q_ROWS = 512
MERGE_ROWS = 512
FFN_CHUNK = 1024

DN_CHUNK = 64
DN_BLOCK = 256
DN_HEADS_PER_STEP = 4
DN_GROUPS = DN_HEADS // DN_HEADS_PER_STEP
DN_ALPHA_LANE = 8
DN_GATE_ROWS = 2 * DN_ALPHA_LANE
DN_COL_BETA, DN_COL_GC, DN_COL_TOT, DN_COL_EG, DN_COL_EKD, DN_COL_ETOT = 0, 1, 3, 4, 6, 8

NT_DIMS = (((1,), (1,)), ((), ()))
TN_DIMS = (((0,), (0,)), ((), ()))
NEG_BIG = -1e30


def _rms(x, w):
    return (x * lax.rsqrt(jnp.mean(x * x, axis=-1, keepdims=True) + EPS)) * w


def _sigmoid(x):
    return 1.0 / (1.0 + jnp.exp(-x))


def _softplus(x):
    return jnp.maximum(x, 0.0) + jnp.log(1.0 + jnp.exp(-jnp.abs(x)))


def _params(semantics):
    return pltpu.CompilerParams(dimension_semantics=semantics, vmem_limit_bytes=V7X_VMEM_LIMIT)


def _head_norm_rope(x, w, cos, sin):
    lane = lax.broadcasted_iota(jnp.int32, (1, LANES), 1)
    low = lane < ATT_HEAD_DIM
    first = (lane & (2 * ROPE_PAIRS - 1)) < ROPE_PAIRS
    sq = x * x
    ms_lo = jnp.sum(jnp.where(low, sq, 0.0), axis=-1, keepdims=True)
    ms_hi = jnp.sum(jnp.where(low, 0.0, sq), axis=-1, keepdims=True)
    ms = jnp.where(low, ms_lo, ms_hi) * (1.0 / ATT_HEAD_DIM)
    y = (x * lax.rsqrt(ms + EPS)) * w
    rot = jnp.where(first, -pltpu.roll(y, LANES - ROPE_PAIRS, 1), pltpu.roll(y, ROPE_PAIRS, 1))
    return y * cos + rot * sin


def _inproj_kernel(x_ref, nw_ref, w_ref, cos_ref, sin_ref, qw_ref, kw_ref, proj_ref, ba_ref, *,
                   n_proj, col_chunk, q_cols, k_cols):
    h = _rms(x_ref[...], nw_ref[...]).astype(BF16)
    has_rope = lambda c0: any(max(c0, lo) < min(c0 + col_chunk, hi) for lo, hi in (q_cols, k_cols))
    for c0 in sorted(range(0, n_proj, col_chunk), key=lambda c0: not has_rope(c0)):
        c1 = min(c0 + col_chunk, n_proj)
        acc = lax.dot_general(h, w_ref[c0:c1, :], NT_DIMS, preferred_element_type=F32)
        for b0 in range(c0, c1, LANES):
            blk = acc[:, b0 - c0:b0 - c0 + LANES]
            if q_cols[0] <= b0 < q_cols[1]:
                blk = _head_norm_rope(blk, qw_ref[...], cos_ref[...], sin_ref[...]) * (ATT_HEAD_DIM ** -0.5 * LOG2_E)
            elif k_cols[0] <= b0 < k_cols[1]:
                blk = _head_norm_rope(blk, kw_ref[...], cos_ref[...], sin_ref[...])
            proj_ref[:, b0:b0 + LANES] = blk.astype(BF16)
    ba_ref[...] = lax.dot_general(h, w_ref[n_proj:, :], NT_DIMS, preferred_element_type=F32)


def _inproj(x2, norm_w, w_prep, cos_t, sin_t, qw, kw, n_proj, q_cols, k_cols, tm):
    t, d = x2.shape
    n_all = w_prep.shape[0]
    n_pos = cos_t.shape[0] // tm
    return pl.pallas_call(
        functools.partial(_inproj_kernel, n_proj=n_proj, col_chunk=INPROJ_COL_CHUNK, q_cols=q_cols, k_cols=k_cols),
        grid=(t // tm,),
        in_specs=[
            pl.BlockSpec((tm, d), lambda i: (i, 0)),
            pl.BlockSpec((1, d), lambda i: (0, 0)),
            pl.BlockSpec((n_all, d), lambda i: (0, 0)),
            pl.BlockSpec((tm, LANES), lambda i: (i % n_pos, 0)),
            pl.BlockSpec((tm, LANES), lambda i: (i % n_pos, 0)),
            pl.BlockSpec((1, LANES), lambda i: (0, 0)),
            pl.BlockSpec((1, LANES), lambda i: (0, 0)),
        ],
        out_specs=[
            pl.BlockSpec((tm, n_proj), lambda i: (i, 0)),
            pl.BlockSpec((tm, n_all - n_proj), lambda i: (i, 0)),
        ],
        out_shape=[
            jax.ShapeDtypeStruct((t, n_proj), BF16),
            jax.ShapeDtypeStruct((t, n_all - n_proj), F32),
        ],
        compiler_params=_params(("arbitrary",)),
        name="inproj",
    )(x2, norm_w, w_prep, cos_t, sin_t, qw, kw)


def _deltanet_kernel(dq_ref, dk_ref, dv_ref, z_ref, ba_ref, wq_ref, wk_ref, wv_ref, gpar_ref, nw_ref,
                     y_ref, col_s, row_s, q_s, k_s, v_s, o_s, st_s, mask_s, bd_s, *pipe_s, seq, hp):
    c = DN_CHUNK
    blk = DN_BLOCK
    n_blk = seq // blk
    per_blk = blk // c
    hd = DN_HEAD_DIM

    gates = ba_ref[...].T[0:DN_GATE_ROWS, :]
    beta = _sigmoid(gates)
    g = -jnp.exp(gpar_ref[0][:, 0:1]) * _softplus(gates + gpar_ref[1][:, 0:1])
    lin = lax.broadcasted_iota(jnp.int32, (DN_GATE_ROWS, seq), 1) & (c - 1)
    pre = g
    suf = g
    sh = 1
    while sh < c:
        pre = pre + jnp.where(lin >= sh, pltpu.roll(pre, sh, 1), 0.0)
        suf = suf + jnp.where(lin < c - sh, pltpu.roll(suf, seq - sh, 1), 0.0)
        sh *= 2
    tot = pre + suf - g
    row_s[0] = pre
    row_s[1] = suf
    a0 = DN_ALPHA_LANE
    packed = [beta[0:a0], pre[a0:], suf[a0:], tot[a0:], jnp.exp(pre[a0:]), jnp.exp(suf[a0:]),
              jnp.exp(tot[a0:] - pre[a0:]), jnp.exp(tot[a0:] - suf[a0:]), jnp.exp(tot[a0:])]
    packed.append(jnp.zeros((LANES - a0 * len(packed), seq), F32))
    col_s[...] = jnp.concatenate(packed, axis=0).T

    ii = lax.broadcasted_iota(jnp.int32, (c, blk), 0)
    jj = lax.broadcasted_iota(jnp.int32, (c, blk), 1) & (c - 1)
    mask_s[0] = jnp.where(ii >= jj, 0.0, NEG_BIG)
    mask_s[1] = jnp.where(ii <= jj, 0.0, NEG_BIG)
    mask_s[2] = jnp.where(ii != jj, 1.0, 0.0)
    bi = lax.broadcasted_iota(jnp.int32, (blk, blk), 0) // c
    bj = lax.broadcasted_iota(jnp.int32, (blk, blk), 1) // c
    bd_s[...] = jnp.where(bi == bj, 1.0, 0.0).astype(BF16)
    lane_chunk = lax.broadcasted_iota(jnp.int32, (1, blk), 1) // c

    tpos = lax.broadcasted_iota(jnp.int32, (seq, hd), 0)

    def conv_silu(x_ref, w_ref, hh):
        x = x_ref[:, hh * hd:(hh + 1) * hd].astype(F32)
        w = w_ref[:, hh * hd:(hh + 1) * hd]
        y = x * w[2:3, :]
        y = y + jnp.where(tpos >= 2, pltpu.roll(x, 2, 0), 0.0) * w[0:1, :]
        y = y + jnp.where(tpos >= 1, pltpu.roll(x, 1, 0), 0.0) * w[1:2, :]
        y = y + jnp.where(tpos < seq - 1, pltpu.roll(x, seq - 1, 0), 0.0) * w[3:4, :]
        y = y + jnp.where(tpos < seq - 2, pltpu.roll(x, seq - 2, 0), 0.0) * w[4:5, :]
        return y * _sigmoid(y)

    def l2n(x):
        return x * lax.rsqrt(jnp.sum(x * x, axis=-1, keepdims=True) + EPS)

    for hh in range(hp):
        q_s[hh] = l2n(conv_silu(dq_ref, wq_ref, hh)) * (hd ** -0.5)
        k_s[hh] = l2n(conv_silu(dk_ref, wk_ref, hh))
        v_s[hh] = conv_silu(dv_ref, wv_ref, hh)
    o_s[...] = jnp.zeros(o_s.shape, F32)
    st_s[...] = jnp.zeros(st_s.shape, F32)

    n_double = c.bit_length() - 2

    chains = [(hh, d) for hh in range(hp) for d in range(2)]

    def diag_wide(a):
        w = a[(per_blk - 1) * c:]
        for i in range(per_blk - 2, -1, -1):
            w = jnp.where(lane_chunk == i, a[i * c:(i + 1) * c], w)
        return w

    def to_blockdiag16(w):
        return jnp.concatenate([w.astype(BF16)] * per_blk, axis=0) * bd_s[...]

    def block_rows(r, d):
        return pl.multiple_of((r if d == 0 else n_blk - 1 - r) * blk, blk)

    def prescan_stages(r, slot):
        pu, pw, pq, pk, pa = slot
        ld = []

        def load():
            for hh, d in chains:
                r0 = block_rows(r, d)
                ci = d * hp + hh
                cols = col_s[pl.ds(r0, blk), :]
                col = lambda group: cols[:, DN_ALPHA_LANE * group + ci:DN_ALPHA_LANE * group + ci + 1]
                e = dict(d=d, q=q_s[hh, pl.ds(r0, blk), :], k=k_s[hh, pl.ds(r0, blk), :],
                         v=v_s[hh, pl.ds(r0, blk), :], bcol=col(DN_COL_BETA), gcol=col(DN_COL_GC + d),
                         eg=col(DN_COL_EG + d), ekd=col(DN_COL_EKD + d),
                         grow=row_s[d, DN_ALPHA_LANE + ci:DN_ALPHA_LANE + ci + 1, pl.ds(r0, blk)])
                e["dec"] = jnp.exp((diag_wide(e["gcol"]) - e["grow"]) + mask_s[d])
                e["kb"] = e["k"] * e["bcol"]
                e["k16"] = e["k"].astype(BF16)
                ld.append(e)

        def kk_mm():
            for e in ld:
                e["kk"] = lax.dot_general(e["kb"].astype(BF16), e["k16"], NT_DIMS, preferred_element_type=F32)

        def qk_mm():
            for e in ld:
                e["qk"] = lax.dot_general(e["q"].astype(BF16), e["k16"], NT_DIMS, preferred_element_type=F32)

        def masked():
            for i, e in enumerate(ld):
                e["l_w"] = diag_wide(e["kk"]) * e["dec"] * mask_s[2]
                pa[i] = (diag_wide(e["qk"]) * e["dec"]).astype(BF16)

        def square():
            for e in ld:
                e["x"] = -e["l_w"]
                e["m"] = jnp.dot(e["l_w"].astype(BF16), to_blockdiag16(e["l_w"]), preferred_element_type=F32)

        def double_mm(last):
            for e in ld:
                x16 = e["x"].astype(BF16)
                lhs = x16 if last else jnp.concatenate([x16, e["m"].astype(BF16)], axis=0)
                e["both"] = jnp.dot(lhs, to_blockdiag16(e["m"]), preferred_element_type=F32)

        def double_add(last):
            for e in ld:
                e["x"] = e["x"] + e["m"] + e["both"][:c]
                if not last:
                    e["m"] = e["both"][c:]

        def make_rhs():
            for e in ld:
                e["rhs"] = jnp.concatenate([e["v"] * e["bcol"], e["kb"] * e["eg"]], axis=1)

        def uw_mm():
            for e in ld:
                e["uw"] = e["rhs"] + jnp.dot(to_blockdiag16(e["x"]), e["rhs"].astype(BF16),
                                             preferred_element_type=F32)

        def store():
            for i, e in enumerate(ld):
                pu[i] = e["uw"][:, :hd]
                pw[i] = e["uw"][:, hd:].astype(BF16)
                pq[i] = (e["q"] * e["eg"]).astype(BF16)
                pk[i] = (e["k"] * e["ekd"]).astype(BF16)

        stages = [load, kk_mm, qk_mm, masked, square]
        for it in range(n_double):
            last = it == n_double - 1
            stages += [functools.partial(double_mm, last), functools.partial(double_add, last)]
        return stages + [make_rhs, uw_mm, store]

    def scan_stages(r, slot):
        pu, pw, pq, pk, pa = slot
        ld = []

        def load():
            for i, (hh, d) in enumerate(chains):
                r0 = block_rows(r, d)
                lane = DN_ALPHA_LANE * DN_COL_ETOT + d * hp + hh
                ld.append(dict(i=i, hh=hh, d=d, r0=r0, etot=col_s[pl.ds(r0, blk), :][:, lane:lane + 1],
                               state=st_s[i]))

        def ws_mm(step):
            for e in ld:
                ci = step if e["d"] == 0 else per_blk - 1 - step
                lo, hi = ci * c, (ci + 1) * c
                e["lo"], e["hi"] = lo, hi
                lhs = jnp.concatenate([pw[e["i"], lo:hi, :], pq[e["i"], lo:hi, :]], axis=0)
                e["ws"] = jnp.dot(lhs, e["state"].astype(BF16), preferred_element_type=F32)

        def av_mm(step):
            for e in ld:
                lo, hi = e["lo"], e["hi"]
                e["vn16"] = (pu[e["i"], lo:hi, :] - e["ws"][:c]).astype(BF16)
                e["av"] = jnp.dot(pa[e["i"]][:, lo:hi], e["vn16"], preferred_element_type=F32)

        def kv_mm(step):
            for e in ld:
                lo, hi = e["lo"], e["hi"]
                e["kv"] = lax.dot_general(pk[e["i"], lo:hi, :], e["vn16"], TN_DIMS, preferred_element_type=F32)

        def update(step):
            for e in ld:
                lo = e["lo"]
                e["state"] = e["state"] * e["etot"][lo:lo + 1, :] + e["kv"]
                rows = pl.ds(e["r0"] + lo, c)
                o_s[e["hh"], rows, :] = o_s[e["hh"], rows, :] + (e["ws"][c:] + e["av"])

        def store():
            for e in ld:
                st_s[e["i"]] = e["state"]

        stages = [load]
        for step in range(per_blk):
            stages += [functools.partial(f, step) for f in (ws_mm, av_mm, kv_mm, update)]
        return stages + [store]

    def run_interleaved(*stage_lists):
        for k in range(max(len(s) for s in stage_lists)):
            for s in stage_lists:
                if k < len(s):
                    s[k]()

    slot_a, slot_b = pipe_s[:5], pipe_s[5:]
    run_interleaved(prescan_stages(0, slot_a))

    def two_block_steps(j, carry):
        r = 2 * j
        run_interleaved(prescan_stages(r + 1, slot_b), scan_stages(r, slot_a))
        run_interleaved(prescan_stages(jnp.minimum(r + 2, n_blk - 1), slot_a), scan_stages(r + 1, slot_b))
        return carry

    lax.fori_loop(0, n_blk // 2, two_block_steps, 0)

    for hh in range(hp):
        zz = z_ref[:, hh * hd:(hh + 1) * hd].astype(F32)
        y_ref[:, hh * hd:(hh + 1) * hd] = (_rms(o_s[hh], nw_ref[...]) * (zz * _sigmoid(zz))).astype(BF16)


def _deltanet(proj, ba, conv_w, gpar, norm_w, batch, seq, col0):
    hp = DN_HEADS_PER_STEP
    wblk = hp * DN_HEAD_DIM
    base = col0 * LANES // wblk
    per = DN_HEADS // hp
    t = batch * seq
    n_chain = 2 * hp
    return pl.pallas_call(
        functools.partial(_deltanet_kernel, seq=seq, hp=hp),
        grid=(batch, per),
        in_specs=[
            pl.BlockSpec((seq, wblk), lambda b, g: (b, base + g)),
            pl.BlockSpec((seq, wblk), lambda b, g: (b, base + per + g)),
            pl.BlockSpec((seq, wblk), lambda b, g: (b, base + 2 * per + g)),
            pl.BlockSpec((seq, wblk), lambda b, g: (b, base + 3 * per + g)),
            pl.BlockSpec((seq, LANES), lambda b, g: (b, g)),
            pl.BlockSpec((CONV_W, wblk), lambda b, g: (0, g)),
            pl.BlockSpec((CONV_W, wblk), lambda b, g: (0, per + g)),
            pl.BlockSpec((CONV_W, wblk), lambda b, g: (0, 2 * per + g)),
            pl.BlockSpec((None, 2, DN_GATE_ROWS, LANES), lambda b, g: (g, 0, 0, 0)),
            pl.BlockSpec((1, DN_HEAD_DIM), lambda b, g: (0, 0)),
        ],
        out_specs=pl.BlockSpec((seq, wblk), lambda b, g: (b, g)),
        out_shape=jax.ShapeDtypeStruct((t, DN_HEADS * DN_HEAD_DIM), BF16),
        scratch_shapes=[
            pltpu.VMEM((seq, LANES), F32),
            pltpu.VMEM((2, DN_GATE_ROWS, seq), F32),
            pltpu.VMEM((hp, seq, DN_HEAD_DIM), F32),
            pltpu.VMEM((hp, seq, DN_HEAD_DIM), F32),
            pltpu.VMEM((hp, seq, DN_HEAD_DIM), F32),
            pltpu.VMEM((hp, seq, DN_HEAD_DIM), F32),
            pltpu.VMEM((n_chain, DN_HEAD_DIM, DN_HEAD_DIM), F32),
            pltpu.VMEM((3, DN_CHUNK, DN_BLOCK), F32),
            pltpu.VMEM((DN_BLOCK, DN_BLOCK), BF16),
        ] + 2 * [
            pltpu.VMEM((n_chain, DN_BLOCK, DN_HEAD_DIM), F32),
            pltpu.VMEM((n_chain, DN_BLOCK, DN_HEAD_DIM), BF16),
            pltpu.VMEM((n_chain, DN_BLOCK, DN_HEAD_DIM), BF16),
            pltpu.VMEM((n_chain, DN_BLOCK, DN_HEAD_DIM), BF16),
            pltpu.VMEM((n_chain, DN_CHUNK, DN_BLOCK), BF16),
        ],
        compiler_params=_params(("arbitrary", "arbitrary")),
        name="deltanet",
    )(proj, proj, proj, proj, ba, conv_w, conv_w, conv_w, gpar, norm_w)


def _attn_kernel(q_ref, k_ref, v_ref, o_ref, klo_s, khi_s, vlo_s, vhi_s):
    g = pl.program_id(1)
    lane = lax.broadcasted_iota(jnp.int32, (1, LANES), 1)
    low = lane < ATT_HEAD_DIM

    @pl.when(pl.program_id(2) == 0)
    def _():
        in_g = (lane >= ATT_HEAD_DIM).astype(jnp.int32) == g
        is0 = g == 0
        for src_ref, lo_s, hi_s, ones_lane in ((k_ref, klo_s, khi_s, None), (v_ref, vlo_s, vhi_s, True)):
            sel = jnp.where(in_g, src_ref[...].astype(F32), 0.0)
            oth = pltpu.roll(sel, ATT_HEAD_DIM, 1)
            lo = jnp.where(is0, sel, oth)
            hi = jnp.where(is0, oth, sel)
            if ones_lane:
                lo = jnp.where(lane == ATT_HEAD_DIM, 1.0, lo)
                hi = jnp.where(lane == 0, 1.0, hi)
            lo_s[...] = lo.astype(BF16)
            hi_s[...] = hi.astype(BF16)

    n_jb = q_ref.shape[1] // LANES
    qbs = [q_ref[:, jb * LANES:(jb + 1) * LANES] for jb in range(n_jb)]
    units = [(jb, k_s, v_s) for jb in range(n_jb) for k_s, v_s in ((klo_s, vlo_s), (khi_s, vhi_s))]
    scores = [lax.dot_general(qbs[jb], k_s[...], NT_DIMS, preferred_element_type=F32) for jb, k_s, _ in units]
    probs = [jnp.exp2(s - jnp.max(s, axis=-1, keepdims=True)).astype(BF16) for s in scores]
    outs = [jnp.dot(p, v_s[...], preferred_element_type=F32) for p, (_, _, v_s) in zip(probs, units)]
    for jb in range(n_jb):
        o_lo, o_hi = outs[2 * jb], outs[2 * jb + 1]
        o_lo = o_lo / o_lo[:, ATT_HEAD_DIM:ATT_HEAD_DIM + 1]
        o_hi = o_hi / o_hi[:, 0:1]
        o_ref[:, jb * LANES:(jb + 1) * LANES] = jnp.where(low, o_lo, o_hi).astype(BF16)


def _attention(proj, batch, seq, q_col0, k_col, v_col, tq):
    t = batch * seq
    q_w = ATT_HEADS * ATT_HEAD_DIM // ATT_KV_HEADS
    q_base = q_col0 * LANES // q_w
    n_q = seq // tq
    return pl.pallas_call(
        _attn_kernel,
        grid=(batch, ATT_KV_HEADS, n_q),
        in_specs=[
            pl.BlockSpec((tq, q_w), lambda b, g, i: (b * n_q + i, q_base + g)),
            pl.BlockSpec((seq, LANES), lambda b, g, i: (b, k_col)),
            pl.BlockSpec((seq, LANES), lambda b, g, i: (b, v_col)),
        ],
        out_specs=pl.BlockSpec((tq, q_w), lambda b, g, i: (b * n_q + i, g)),
        out_shape=jax.ShapeDtypeStruct((t, ATT_HEADS * ATT_HEAD_DIM), BF16),
        scratch_shapes=[pltpu.VMEM((seq, LANES), BF16) for _ in range(4)],
        compiler_params=_params(("arbitrary", "arbitrary", "arbitrary")),
        name="gqa",
    )(proj, proj, proj)


def _merge_ffn_kernel(x_ref, g_ref, ydn_ref, yatt_ref, xq_ref, mem_ref, mw_ref, wkv_ref, wb_ref, wo_ref, nw_ref,
                      wu_ref, wd_ref, fw_ref, o_ref, mk_s, mv_s, *, ff_chunk, final_norm, n_pos):
    d = x_ref.shape[1]
    xw = X_HEADS * X_HEAD_DIM

    @pl.when(pl.program_id(0) % n_pos == 0)
    def _():
        hm = _rms(mem_ref[...], mw_ref[...]).astype(BF16)
        kv = jnp.dot(hm, wkv_ref[...], preferred_element_type=F32)
        mk_s[...] = kv[:, :xw].astype(BF16)
        mv_s[...] = kv[:, xw:].astype(BF16)

    heads = [slice(h * X_HEAD_DIM, (h + 1) * X_HEAD_DIM) for h in range(X_HEADS)]
    scores = [lax.dot_general(xq_ref[:, cols], mk_s[:, cols], NT_DIMS, preferred_element_type=F32)
              * (X_HEAD_DIM ** -0.5 * LOG2_E) for cols in heads]
    probs = [jnp.exp2(s - jnp.max(s, axis=-1, keepdims=True)) for s in scores]
    dens = [jnp.sum(p, axis=-1, keepdims=True) for p in probs]
    outs = [jnp.dot(p.astype(BF16), mv_s[:, cols], preferred_element_type=F32) for p, cols in zip(probs, heads)]
    yx = jnp.concatenate([(o / den).astype(BF16) for o, den in zip(outs, dens)], axis=1)

    merged = None
    for n, y in enumerate((ydn_ref[...], yatt_ref[...], yx)):
        yb = jnp.dot(y, wb_ref[n], preferred_element_type=F32)
        term = _sigmoid(g_ref[:, n * d:(n + 1) * d].astype(F32)) * yb
        merged = term if merged is None else merged + term
    x = x_ref[...] + jnp.dot(merged.astype(BF16), wo_ref[...], preferred_element_type=F32)
    h = _rms(x, nw_ref[...]).astype(BF16)
    acc = x
    for c0 in range(0, wu_ref.shape[1], ff_chunk):
        u = jnp.dot(h, wu_ref[:, c0:c0 + ff_chunk], preferred_element_type=F32)
        a = jnp.square(jnp.maximum(u, 0.0)).astype(BF16)
        acc = acc + jnp.dot(a, wd_ref[c0:c0 + ff_chunk, :], preferred_element_type=F32)
    o_ref[...] = _rms(acc, fw_ref[...]) if final_norm else acc


def _merge_ffn(x2, proj, ydn, yatt, mem2, mem_norm_w, w_kv, w_branch, w_out, norm_w, w_up, w_down, final_w,
               final_norm, seq, m_len, xq_col0, tm):
    t, d = x2.shape
    bw = ydn.shape[1]
    dff = w_up.shape[1]
    n_pos = seq // tm
    xq_base = xq_col0 * LANES // bw
    resident = dict(pipeline_mode=pl.Buffered(1))
    return pl.pallas_call(
        functools.partial(_merge_ffn_kernel, ff_chunk=FFN_CHUNK, final_norm=final_norm, n_pos=n_pos),
        grid=(t // tm,),
        in_specs=[
            pl.BlockSpec((tm, d), lambda i: (i, 0)),
            pl.BlockSpec((tm, N_BRANCH * d), lambda i: (i, 0)),
            pl.BlockSpec((tm, bw), lambda i: (i, 0)),
            pl.BlockSpec((tm, bw), lambda i: (i, 0)),
            pl.BlockSpec((tm, bw), lambda i: (i, xq_base)),
            pl.BlockSpec((m_len, d), lambda i: (i // n_pos, 0)),
            pl.BlockSpec((1, d), lambda i: (0, 0)),
            pl.BlockSpec((d, 2 * bw), lambda i: (0, 0), **resident),
            pl.BlockSpec((N_BRANCH, bw, d), lambda i: (0, 0, 0), **resident),
            pl.BlockSpec((d, d), lambda i: (0, 0), **resident),
            pl.BlockSpec((1, d), lambda i: (0, 0)),
            pl.BlockSpec((d, dff), lambda i: (0, 0), **resident),
            pl.BlockSpec((dff, d), lambda i: (0, 0), **resident),
            pl.BlockSpec((1, d), lambda i: (0, 0)),
        ],
        out_specs=pl.BlockSpec((tm, d), lambda i: (i, 0)),
        out_shape=jax.ShapeDtypeStruct((t, d), F32),
        scratch_shapes=[pltpu.VMEM((m_len, bw), BF16), pltpu.VMEM((m_len, bw), BF16)],
        compiler_params=_params(("arbitrary",)),
        name="merge_ffn",
    )(x2, proj, ydn, yatt, proj, mem2, mem_norm_w, w_kv, w_branch, w_out, norm_w, w_up, w_down, final_w)


def _gate_lanes(beta_like, alpha_like):
    hp = DN_HEADS_PER_STEP
    lead = beta_like.shape[:-1]
    blocks = []
    for g in range(DN_GROUPS):
        pick = lambda a: [a[..., dd * DN_HEADS + g * hp: dd * DN_HEADS + (g + 1) * hp] for dd in range(2)]
        blocks += pick(beta_like) + [jnp.zeros(lead + (DN_ALPHA_LANE - 2 * hp,), beta_like.dtype)]
        blocks += pick(alpha_like) + [jnp.zeros(lead + (LANES - DN_ALPHA_LANE - 2 * hp,), beta_like.dtype)]
    return jnp.concatenate(blocks, axis=-1)


def _rope_tables(seq):
    pos = np.arange(seq)
    inv_freq = (ROPE_THETA ** (-np.arange(ROPE_PAIRS, dtype=np.float32) / ROPE_PAIRS)).astype(np.float32)
    ang_r = (pos // GRID_W).astype(np.float32)[:, None] * inv_freq
    ang_c = (pos % GRID_W).astype(np.float32)[:, None] * inv_freq
    ang = np.concatenate([ang_r, ang_r, ang_c, ang_c] * (LANES // ATT_HEAD_DIM), axis=1)
    return jnp.asarray(np.cos(ang), F32), jnp.asarray(np.sin(ang), F32)


def kernel(x, mem, mix_norm_w, w_in, dn_conv_w, dn_a_log, dn_dt_bias, dn_norm_w, q_norm_w, k_norm_w,
           mem_norm_w, w_mem_kv, w_branch, w_out, ffn_norm_w, w_up, w_down, final_norm_w):
    batch, seq, d = x.shape
    m_len = mem.shape[1]
    depth = w_in.shape[0]
    bw = d // 2
    dn_cols = 3 * bw
    assert seq % (2 * DN_BLOCK) == 0 and DN_HEADS * DN_HEAD_DIM == bw
    assert seq % INPROJ_ROWS == 0 and seq % ATT_Q_ROWS == 0 and seq % MERGE_ROWS == 0
    assert ATT_HEADS * ATT_HEAD_DIM == bw and X_HEADS * X_HEAD_DIM == bw

    n_gate = N_BRANCH * d
    col_dn = n_gate // LANES
    col_aq = col_dn + (dn_cols + bw) // LANES
    col_xq = col_aq + bw // LANES
    col_ak = col_xq + bw // LANES
    col_av = col_ak + 1
    n_proj = (col_av + 1) * LANES
    o_qkv, o_z, o_b, o_a = 0, dn_cols, dn_cols + bw, dn_cols + bw + 2 * DN_HEADS
    o_aq = o_a + 2 * DN_HEADS
    o_ak = o_aq + bw
    o_av = o_ak + ATT_KV_HEADS * ATT_HEAD_DIM
    o_xq = o_av + ATT_KV_HEADS * ATT_HEAD_DIM
    o_g = o_xq + bw

    cos_t, sin_t = _rope_tables(seq)
    x2 = x.reshape(batch * seq, d)
    mem2 = mem.reshape(batch * m_len, d)
    tile = lambda w: jnp.tile(w, LANES // ATT_HEAD_DIM)[None, :]

    for l in range(depth):
        wt = jnp.swapaxes(w_in[l], 0, 1).astype(BF16)
        w_prep = jnp.concatenate(
            [wt[o_g:], wt[o_qkv:o_b], wt[o_aq:o_ak], wt[o_xq:o_g], wt[o_ak:o_xq],
             _gate_lanes(wt[o_b:o_a].T, wt[o_a:o_aq].T).T], axis=0)
        zeros8 = jnp.zeros((2 * DN_HEADS,), F32)
        gate_rows = lambda p: jnp.broadcast_to(
            _gate_lanes(zeros8, p.reshape(-1)).reshape(DN_GROUPS, 1, LANES)[:, :, :DN_GATE_ROWS, None],
            (DN_GROUPS, 1, DN_GATE_ROWS, LANES))
        gpar = jnp.concatenate([gate_rows(dn_a_log[l]), gate_rows(dn_dt_bias[l])], axis=1)

        proj, ba = _inproj(x2, mix_norm_w[l][None, :], w_prep, cos_t, sin_t, tile(q_norm_w[l]), tile(k_norm_w[l]),
                           n_proj, q_cols=(col_aq * LANES, col_xq * LANES), k_cols=(col_ak * LANES, col_av * LANES),
                           tm=INPROJ_ROWS)
        ydn = _deltanet(proj, ba, dn_conv_w[l], gpar, dn_norm_w[l][None, :], batch, seq, col_dn)
        yatt = _attention(proj, batch, seq, col_aq, col_ak, col_av, tq=ATT_Q_ROWS)
        x2 = _merge_ffn(x2, proj, ydn, yatt, mem2, mem_norm_w[l][None, :], w_mem_kv[l].astype(BF16),
                        w_branch[l].astype(BF16), w_out[l].astype(BF16),
                        ffn_norm_w[l][None, :], w_up[l].astype(BF16), w_down[l].astype(BF16),
                        final_norm_w[None, :], final_norm=(l == depth - 1), seq=seq, m_len=m_len, xq_col0=col_xq,
                        tm=MERGE_ROWS)
    return x2.reshape(batch, seq, d)
```

```python
import functools

import numpy as np
import jax
import jax.numpy as jnp
from jax import lax
from jax.experimental import pallas as pl
from jax.experimental.pallas import tpu as pltpu

F32 = jnp.float32
BF16 = jnp.bfloat16

DN_HEADS = 4
DN_HEAD_DIM = 128
CONV_W = 5
ATT_HEADS = 8
ATT_KV_HEADS = 2
ATT_HEAD_DIM = 64
ROPE_PAIRS = ATT_HEAD_DIM // 4
ROPE_THETA = 10000.0
GRID_W = 64
X_HEADS = 4
X_HEAD_DIM = 128
N_BRANCH = 3
EPS = 1e-6
LOG2_E = 1.4426950408889634

LANES = 128
V7X_VMEM_LIMIT = 58 * 1024 * 1024

INPROJ_ROWS = 512
INPROJ_COL_CHUNK = 512
ATT_Q_ROWS = 512
MERGE_ROWS = 512
FFN_CHUNK = 1024

DN_CHUNK = 64
DN_BLOCK = 256
DN_HEADS_PER_STEP = 4
DN_GROUPS = DN_HEADS // DN_HEADS_PER_STEP
DN_ALPHA_LANE = 8
DN_GATE_ROWS = 2 * DN_ALPHA_LANE
DN_COL_BETA, DN_COL_GC, DN_COL_TOT, DN_COL_EG, DN_COL_EKD, DN_COL_ETOT = 0, 1, 3, 4, 6, 8

NT_DIMS = (((1,), (1,)), ((), ()))
TN_DIMS = (((0,), (0,)), ((), ()))
NEG_BIG = -1e30


def _rms(x, w):
    return (x * lax.rsqrt(jnp.mean(x * x, axis=-1, keepdims=True) + EPS)) * w


def _sigmoid(x):
    return 1.0 / (1.0 + jnp.exp(-x))


def _softplus(x):
    return jnp.maximum(x, 0.0) + jnp.log(1.0 + jnp.exp(-jnp.abs(x)))


def _params(semantics):
    return pltpu.CompilerParams(dimension_semantics=semantics, vmem_limit_bytes=V7X_VMEM_LIMIT)


def _head_norm_rope(x, w, cos, sin):
    lane = lax.broadcasted_iota(jnp.int32, (1, LANES), 1)
    low = lane < ATT_HEAD_DIM
    first = (lane & (2 * ROPE_PAIRS - 1)) < ROPE_PAIRS
    sq = x * x
    ms_lo = jnp.sum(jnp.where(low, sq, 0.0), axis=-1, keepdims=True)
    ms_hi = jnp.sum(jnp.where(low, 0.0, sq), axis=-1, keepdims=True)
    ms = jnp.where(low, ms_lo, ms_hi) * (1.0 / ATT_HEAD_DIM)
    y = (x * lax.rsqrt(ms + EPS)) * w
    rot = jnp.where(first, -pltpu.roll(y, LANES - ROPE_PAIRS, 1), pltpu.roll(y, ROPE_PAIRS, 1))
    return y * cos + rot * sin


def _inproj_kernel(x_ref, nw_ref, w_ref, cos_ref, sin_ref, qw_ref, kw_ref, proj_ref, ba_ref, *,
                   n_proj, col_chunk, q_cols, k_cols):
    h = _rms(x_ref[...], nw_ref[...]).astype(BF16)
    has_rope = lambda c0: any(max(c0, lo) < min(c0 + col_chunk, hi) for lo, hi in (q_cols, k_cols))
    for c0 in sorted(range(0, n_proj, col_chunk), key=lambda c0: not has_rope(c0)):
        c1 = min(c0 + col_chunk, n_proj)
        acc = lax.dot_general(h, w_ref[c0:c1, :], NT_DIMS, preferred_element_type=F32)
        for b0 in range(c0, c1, LANES):
            blk = acc[:, b0 - c0:b0 - c0 + LANES]
            if q_cols[0] <= b0 < q_cols[1]:
                blk = _head_norm_rope(blk, qw_ref[...], cos_ref[...], sin_ref[...]) * (ATT_HEAD_DIM ** -0.5 * LOG2_E)
            elif k_cols[0] <= b0 < k_cols[1]:
                blk = _head_norm_rope(blk, kw_ref[...], cos_ref[...], sin_ref[...])
            proj_ref[:, b0:b0 + LANES] = blk.astype(BF16)
    ba_ref[...] = lax.dot_general(h, w_ref[n_proj:, :], NT_DIMS, preferred_element_type=F32)


def _inproj(x2, norm_w, w_prep, cos_t, sin_t, qw, kw, n_proj, q_cols, k_cols, tm):
    t, d = x2.shape
    n_all = w_prep.shape[0]
    n_pos = cos_t.shape[0] // tm
    return pl.pallas_call(
        functools.partial(_inproj_kernel, n_proj=n_proj, col_chunk=INPROJ_COL_CHUNK, q_cols=q_cols, k_cols=k_cols),
        grid=(t // tm,),
        in_specs=[
            pl.BlockSpec((tm, d), lambda i: (i, 0)),
            pl.BlockSpec((1, d), lambda i: (0, 0)),
            pl.BlockSpec((n_all, d), lambda i: (0, 0)),
            pl.BlockSpec((tm, LANES), lambda i: (i % n_pos, 0)),
            pl.BlockSpec((tm, LANES), lambda i: (i % n_pos, 0)),
            pl.BlockSpec((1, LANES), lambda i: (0, 0)),
            pl.BlockSpec((1, LANES), lambda i: (0, 0)),
        ],
        out_specs=[
            pl.BlockSpec((tm, n_proj), lambda i: (i, 0)),
            pl.BlockSpec((tm, n_all - n_proj), lambda i: (i, 0)),
        ],
        out_shape=[
            jax.ShapeDtypeStruct((t, n_proj), BF16),
            jax.ShapeDtypeStruct((t, n_all - n_proj), F32),
        ],
        compiler_params=_params(("arbitrary",)),
        name="inproj",
    )(x2, norm_w, w_prep, cos_t, sin_t, qw, kw)


def _deltanet_kernel(dq_ref, dk_ref, dv_ref, z_ref, ba_ref, wq_ref, wk_ref, wv_ref, gpar_ref, nw_ref,
                     y_ref, col_s, row_s, q_s, k_s, v_s, o_s, st_s, mask_s, bd_s, *pipe_s, seq, hp):
    c = DN_CHUNK
    blk = DN_BLOCK
    n_blk = seq // blk
    per_blk = blk // c
    hd = DN_HEAD_DIM

    gates = ba_ref[...].T[0:DN_GATE_ROWS, :]
    beta = _sigmoid(gates)
    g = -jnp.exp(gpar_ref[0][:, 0:1]) * _softplus(gates + gpar_ref[1][:, 0:1])
    lin = lax.broadcasted_iota(jnp.int32, (DN_GATE_ROWS, seq), 1) & (c - 1)
    pre = g
    suf = g
    sh = 1
    while sh < c:
        pre = pre + jnp.where(lin >= sh, pltpu.roll(pre, sh, 1), 0.0)
        suf = suf + jnp.where(lin < c - sh, pltpu.roll(suf, seq - sh, 1), 0.0)
        sh *= 2
    tot = pre + suf - g
    row_s[0] = pre
    row_s[1] = suf
    a0 = DN_ALPHA_LANE
    packed = [beta[0:a0], pre[a0:], suf[a0:], tot[a0:], jnp.exp(pre[a0:]), jnp.exp(suf[a0:]),
              jnp.exp(tot[a0:] - pre[a0:]), jnp.exp(tot[a0:] - suf[a0:]), jnp.exp(tot[a0:])]
    packed.append(jnp.zeros((LANES - a0 * len(packed), seq), F32))
    col_s[...] = jnp.concatenate(packed, axis=0).T

    ii = lax.broadcasted_iota(jnp.int32, (c, blk), 0)
    jj = lax.broadcasted_iota(jnp.int32, (c, blk), 1) & (c - 1)
    mask_s[0] = jnp.where(ii >= jj, 0.0, NEG_BIG)
    mask_s[1] = jnp.where(ii <= jj, 0.0, NEG_BIG)
    mask_s[2] = jnp.where(ii != jj, 1.0, 0.0)
    bi = lax.broadcasted_iota(jnp.int32, (blk, blk), 0) // c
    bj = lax.broadcasted_iota(jnp.int32, (blk, blk), 1) // c
    bd_s[...] = jnp.where(bi == bj, 1.0, 0.0).astype(BF16)
    lane_chunk = lax.broadcasted_iota(jnp.int32, (1, blk), 1) // c

    tpos = lax.broadcasted_iota(jnp.int32, (seq, hd), 0)

    def conv_silu(x_ref, w_ref, hh):
        x = x_ref[:, hh * hd:(hh + 1) * hd].astype(F32)
        w = w_ref[:, hh * hd:(hh + 1) * hd]
        y = x * w[2:3, :]
        y = y + jnp.where(tpos >= 2, pltpu.roll(x, 2, 0), 0.0) * w[0:1, :]
        y = y + jnp.where(tpos >= 1, pltpu.roll(x, 1, 0), 0.0) * w[1:2, :]
        y = y + jnp.where(tpos < seq - 1, pltpu.roll(x, seq - 1, 0), 0.0) * w[3:4, :]
        y = y + jnp.where(tpos < seq - 2, pltpu.roll(x, seq - 2, 0), 0.0) * w[4:5, :]
        return y * _sigmoid(y)

    def l2n(x):
        return x * lax.rsqrt(jnp.sum(x * x, axis=-1, keepdims=True) + EPS)

    for hh in range(hp):
        q_s[hh] = l2n(conv_silu(dq_ref, wq_ref, hh)) * (hd ** -0.5)
        k_s[hh] = l2n(conv_silu(dk_ref, wk_ref, hh))
        v_s[hh] = conv_silu(dv_ref, wv_ref, hh)
    o_s[...] = jnp.zeros(o_s.shape, F32)
    st_s[...] = jnp.zeros(st_s.shape, F32)

    n_double = c.bit_length() - 2

    chains = [(hh, d) for hh in range(hp) for d in range(2)]

    def diag_wide(a):
        w = a[(per_blk - 1) * c:]
        for i in range(per_blk - 2, -1, -1):
            w = jnp.where(lane_chunk == i, a[i * c:(i + 1) * c], w)
        return w

    def to_blockdiag16(w):
        return jnp.concatenate([w.astype(BF16)] * per_blk, axis=0) * bd_s[...]

    def block_rows(r, d):
        return pl.multiple_of((r if d == 0 else n_blk - 1 - r) * blk, blk)

    def prescan_stages(r, slot):
        pu, pw, pq, pk, pa = slot
        ld = []

        def load():
            for hh, d in chains:
                r0 = block_rows(r, d)
                ci = d * hp + hh
                cols = col_s[pl.ds(r0, blk), :]
                col = lambda group: cols[:, DN_ALPHA_LANE * group + ci:DN_ALPHA_LANE * group + ci + 1]
                e = dict(d=d, q=q_s[hh, pl.ds(r0, blk), :], k=k_s[hh, pl.ds(r0, blk), :],
                         v=v_s[hh, pl.ds(r0, blk), :], bcol=col(DN_COL_BETA), gcol=col(DN_COL_GC + d),
                         eg=col(DN_COL_EG + d), ekd=col(DN_COL_EKD + d),
                         grow=row_s[d, DN_ALPHA_LANE + ci:DN_ALPHA_LANE + ci + 1, pl.ds(r0, blk)])
                e["dec"] = jnp.exp((diag_wide(e["gcol"]) - e["grow"]) + mask_s[d])
                e["kb"] = e["k"] * e["bcol"]
                e["k16"] = e["k"].astype(BF16)
                ld.append(e)

        def kk_mm():
            for e in ld:
                e["kk"] = lax.dot_general(e["kb"].astype(BF16), e["k16"], NT_DIMS, preferred_element_type=F32)

        def qk_mm():
            for e in ld:
                e["qk"] = lax.dot_general(e["q"].astype(BF16), e["k16"], NT_DIMS, preferred_element_type=F32)

        def masked():
            for i, e in enumerate(ld):
                e["l_w"] = diag_wide(e["kk"]) * e["dec"] * mask_s[2]
                pa[i] = (diag_wide(e["qk"]) * e["dec"]).astype(BF16)

        def square():
            for e in ld:
                e["x"] = -e["l_w"]
                e["m"] = jnp.dot(e["l_w"].astype(BF16), to_blockdiag16(e["l_w"]), preferred_element_type=F32)

        def double_mm(last):
            for e in ld:
                x16 = e["x"].astype(BF16)
                lhs = x16 if last else jnp.concatenate([x16, e["m"].astype(BF16)], axis=0)
                e["both"] = jnp.dot(lhs, to_blockdiag16(e["m"]), preferred_element_type=F32)

        def double_add(last):
            for e in ld:
                e["x"] = e["x"] + e["m"] + e["both"][:c]
                if not last:
                    e["m"] = e["both"][c:]

        def make_rhs():
            for e in ld:
                e["rhs"] = jnp.concatenate([e["v"] * e["bcol"], e["kb"] * e["eg"]], axis=1)

        def uw_mm():
            for e in ld:
                e["uw"] = e["rhs"] + jnp.dot(to_blockdiag16(e["x"]), e["rhs"].astype(BF16),
                                             preferred_element_type=F32)

        def store():
            for i, e in enumerate(ld):
                pu[i] = e["uw"][:, :hd]
                pw[i] = e["uw"][:, hd:].astype(BF16)
                pq[i] = (e["q"] * e["eg"]).astype(BF16)
                pk[i] = (e["k"] * e["ekd"]).astype(BF16)

        stages = [load, kk_mm, qk_mm, masked, square]
        for it in range(n_double):
            last = it == n_double - 1
            stages += [functools.partial(double_mm, last), functools.partial(double_add, last)]
        return stages + [make_rhs, uw_mm, store]

    def scan_stages(r, slot):
        pu, pw, pq, pk, pa = slot
        ld = []

        def load():
            for i, (hh, d) in enumerate(chains):
                r0 = block_rows(r, d)
                lane = DN_ALPHA_LANE * DN_COL_ETOT + d * hp + hh
                ld.append(dict(i=i, hh=hh, d=d, r0=r0, etot=col_s[pl.ds(r0, blk), :][:, lane:lane + 1],
                               state=st_s[i]))

        def ws_mm(step):
            for e in ld:
                ci = step if e["d"] == 0 else per_blk - 1 - step
                lo, hi = ci * c, (ci + 1) * c
                e["lo"], e["hi"] = lo, hi
                lhs = jnp.concatenate([pw[e["i"], lo:hi, :], pq[e["i"], lo:hi, :]], axis=0)
                e["ws"] = jnp.dot(lhs, e["state"].astype(BF16), preferred_element_type=F32)

        def av_mm(step):
            for e in ld:
                lo, hi = e["lo"], e["hi"]
                e["vn16"] = (pu[e["i"], lo:hi, :] - e["ws"][:c]).astype(BF16)
                e["av"] = jnp.dot(pa[e["i"]][:, lo:hi], e["vn16"], preferred_element_type=F32)

        def kv_mm(step):
            for e in ld:
                lo, hi = e["lo"], e["hi"]
                e["kv"] = lax.dot_general(pk[e["i"], lo:hi, :], e["vn16"], TN_DIMS, preferred_element_type=F32)

        def update(step):
            for e in ld:
                lo = e["lo"]
                e["state"] = e["state"] * e["etot"][lo:lo + 1, :] + e["kv"]
                rows = pl.ds(e["r0"] + lo, c)
                o_s[e["hh"], rows, :] = o_s[e["hh"], rows, :] + (e["ws"][c:] + e["av"])

        def store():
            for e in ld:
                st_s[e["i"]] = e["state"]

        stages = [load]
        for step in range(per_blk):
            stages += [functools.partial(f, step) for f in (ws_mm, av_mm, kv_mm, update)]
        return stages + [store]

    def run_interleaved(*stage_lists):
        for k in range(max(len(s) for s in stage_lists)):
            for s in stage_lists:
                if k < len(s):
                    s[k]()

    slot_a, slot_b = pipe_s[:5], pipe_s[5:]
    run_interleaved(prescan_stages(0, slot_a))

    def two_block_steps(j, carry):
        r = 2 * j
        run_interleaved(prescan_stages(r + 1, slot_b), scan_stages(r, slot_a))
        run_interleaved(prescan_stages(jnp.minimum(r + 2, n_blk - 1), slot_a), scan_stages(r + 1, slot_b))
        return carry

    lax.fori_loop(0, n_blk // 2, two_block_steps, 0)

    for hh in range(hp):
        zz = z_ref[:, hh * hd:(hh + 1) * hd].astype(F32)
        y_ref[:, hh * hd:(hh + 1) * hd] = (_rms(o_s[hh], nw_ref[...]) * (zz * _sigmoid(zz))).astype(BF16)


def _deltanet(proj, ba, conv_w, gpar, norm_w, batch, seq, col0):
    hp = DN_HEADS_PER_STEP
    wblk = hp * DN_HEAD_DIM
    base = col0 * LANES // wblk
    per = DN_HEADS // hp
    t = batch * seq
    n_chain = 2 * hp
    return pl.pallas_call(
        functools.partial(_deltanet_kernel, seq=seq, hp=hp),
        grid=(batch, per),
        in_specs=[
            pl.BlockSpec((seq, wblk), lambda b, g: (b, base + g)),
            pl.BlockSpec((seq, wblk), lambda b, g: (b, base + per + g)),
            pl.BlockSpec((seq, wblk), lambda b, g: (b, base + 2 * per + g)),
            pl.BlockSpec((seq, wblk), lambda b, g: (b, base + 3 * per + g)),
            pl.BlockSpec((seq, LANES), lambda b, g: (b, g)),
            pl.BlockSpec((CONV_W, wblk), lambda b, g: (0, g)),
            pl.BlockSpec((CONV_W, wblk), lambda b, g: (0, per + g)),
            pl.BlockSpec((CONV_W, wblk), lambda b, g: (0, 2 * per + g)),
            pl.BlockSpec((None, 2, DN_GATE_ROWS, LANES), lambda b, g: (g, 0, 0, 0)),
            pl.BlockSpec((1, DN_HEAD_DIM), lambda b, g: (0, 0)),
        ],
        out_specs=pl.BlockSpec((seq, wblk), lambda b, g: (b, g)),
        out_shape=jax.ShapeDtypeStruct((t, DN_HEADS * DN_HEAD_DIM), BF16),
        scratch_shapes=[
            pltpu.VMEM((seq, LANES), F32),
            pltpu.VMEM((2, DN_GATE_ROWS, seq), F32),
            pltpu.VMEM((hp, seq, DN_HEAD_DIM), F32),
            pltpu.VMEM((hp, seq, DN_HEAD_DIM), F32),
            pltpu.VMEM((hp, seq, DN_HEAD_DIM), F32),
            pltpu.VMEM((hp, seq, DN_HEAD_DIM), F32),
            pltpu.VMEM((n_chain, DN_HEAD_DIM, DN_HEAD_DIM), F32),
            pltpu.VMEM((3, DN_CHUNK, DN_BLOCK), F32),
            pltpu.VMEM((DN_BLOCK, DN_BLOCK), BF16),
        ] + 2 * [
            pltpu.VMEM((n_chain, DN_BLOCK, DN_HEAD_DIM), F32),
            pltpu.VMEM((n_chain, DN_BLOCK, DN_HEAD_DIM), BF16),
            pltpu.VMEM((n_chain, DN_BLOCK, DN_HEAD_DIM), BF16),
            pltpu.VMEM((n_chain, DN_BLOCK, DN_HEAD_DIM), BF16),
            pltpu.VMEM((n_chain, DN_CHUNK, DN_BLOCK), BF16),
        ],
        compiler_params=_params(("arbitrary", "arbitrary")),
        name="deltanet",
    )(proj, proj, proj, proj, ba, conv_w, conv_w, conv_w, gpar, norm_w)


def _attn_kernel(q_ref, k_ref, v_ref, o_ref, klo_s, khi_s, vlo_s, vhi_s):
    g = pl.program_id(1)
    lane = lax.broadcasted_iota(jnp.int32, (1, LANES), 1)

    @pl.when(pl.program_id(2) == 0)
    def _():
        in_g = (lane >= ATT_HEAD_DIM).astype(jnp.int32) == g
        is0 = g == 0
        for src_ref, lo_s, hi_s, ones_lane in ((k_ref, klo_s, khi_s, None), (v_ref, vlo_s, vhi_s, True)):
            sel = jnp.where(in_g, src_ref[...].astype(F32), 0.0)
            oth = pltpu.roll(sel, ATT_HEAD_DIM, 1)
            lo = jnp.where(is0, sel, oth)
            hi = jnp.where(is0, oth, sel)
            if ones_lane:
                lo = jnp.where(lane == ATT_HEAD_DIM, 1.0, lo)
                hi = jnp.where(lane == 0, 1.0, hi)
                lo_s[...] = lo.T.astype(BF16)
                hi_s[...] = hi.T.astype(BF16)
            else:
                lo_s[...] = lo.astype(BF16)
                hi_s[...] = hi.astype(BF16)

    n_jb = q_ref.shape[1] // LANES
    qbs = [q_ref[:, jb * LANES:(jb + 1) * LANES] for jb in range(n_jb)]
    units = [(jb, k_s, v_s) for jb in range(n_jb) for k_s, v_s in ((klo_s, vlo_s), (khi_s, vhi_s))]
    scores = [lax.dot_general(k_s[...], qbs[jb], NT_DIMS, preferred_element_type=F32) for jb, k_s, _ in units]
    probs = [jnp.exp2(s - jnp.max(s, axis=0, keepdims=True)).astype(BF16) for s in scores]
    outs = [jnp.dot(v_s[...], p, preferred_element_type=F32) for p, (_, _, v_s) in zip(probs, units)]
    low_rows = lax.broadcasted_iota(jnp.int32, (LANES, 1), 0) < ATT_HEAD_DIM
    for jb in range(n_jb):
        o_lo, o_hi = outs[2 * jb], outs[2 * jb + 1]
        o_lo = o_lo / o_lo[ATT_HEAD_DIM:ATT_HEAD_DIM + 1, :]
        o_hi = o_hi / o_hi[0:1, :]
        o_ref[:, jb * LANES:(jb + 1) * LANES] = jnp.where(low_rows, o_lo, o_hi).T.astype(BF16)


def _attention(proj, batch, seq, q_col0, k_col, v_col, tq):
    t = batch * seq
    q_w = ATT_HEADS * ATT_HEAD_DIM // ATT_KV_HEADS
    q_base = q_col0 * LANES // q_w
    n_q = seq // tq
    return pl.pallas_call(
        _attn_kernel,
        grid=(batch, ATT_KV_HEADS, n_q),
        in_specs=[
            pl.BlockSpec((tq, q_w), lambda b, g, i: (b * n_q + i, q_base + g)),
            pl.BlockSpec((seq, LANES), lambda b, g, i: (b, k_col)),
            pl.BlockSpec((seq, LANES), lambda b, g, i: (b, v_col)),
        ],
        out_specs=pl.BlockSpec((tq, q_w), lambda b, g, i: (b * n_q + i, g)),
        out_shape=jax.ShapeDtypeStruct((t, ATT_HEADS * ATT_HEAD_DIM), BF16),
        scratch_shapes=[pltpu.VMEM((seq, LANES), BF16) for _ in range(2)]
        + [pltpu.VMEM((LANES, seq), BF16) for _ in range(2)],
        compiler_params=_params(("arbitrary", "arbitrary", "arbitrary")),
        name="gqa",
    )(proj, proj, proj)


def _merge_ffn_kernel(x_ref, g_ref, ydn_ref, yatt_ref, xq_ref, mem_ref, mw_ref, wkv_ref, wb_ref, wo_ref, nw_ref,
                      wu_ref, wd_ref, fw_ref, o_ref, mk_s, mv_s, *, ff_chunk, final_norm, n_pos):
    d = x_ref.shape[1]
    xw = X_HEADS * X_HEAD_DIM

    @pl.when(pl.program_id(0) % n_pos == 0)
    def _():
        hm = _rms(mem_ref[...], mw_ref[...]).astype(BF16)
        kv = jnp.dot(hm, wkv_ref[...], preferred_element_type=F32)
        mk_s[...] = kv[:, :xw].astype(BF16)
        mv_s[...] = kv[:, xw:].astype(BF16)

    heads = [slice(h * X_HEAD_DIM, (h + 1) * X_HEAD_DIM) for h in range(X_HEADS)]
    scores = [lax.dot_general(xq_ref[:, cols], mk_s[:, cols], NT_DIMS, preferred_element_type=F32)
              * (X_HEAD_DIM ** -0.5 * LOG2_E) for cols in heads]
    probs = [jnp.exp2(s - jnp.max(s, axis=-1, keepdims=True)) for s in scores]
    dens = [jnp.sum(p, axis=-1, keepdims=True) for p in probs]
    outs = [jnp.dot(p.astype(BF16), mv_s[:, cols], preferred_element_type=F32) for p, cols in zip(probs, heads)]
    yx = jnp.concatenate([(o / den).astype(BF16) for o, den in zip(outs, dens)], axis=1)

    merged = None
    for n, y in enumerate((ydn_ref[...], yatt_ref[...], yx)):
        yb = jnp.dot(y, wb_ref[n], preferred_element_type=F32)
        term = _sigmoid(g_ref[:, n * d:(n + 1) * d].astype(F32)) * yb
        merged = term if merged is None else merged + term
    x = x_ref[...] + jnp.dot(merged.astype(BF16), wo_ref[...], preferred_element_type=F32)
    h = _rms(x, nw_ref[...]).astype(BF16)
    acc = x
    for c0 in range(0, wu_ref.shape[1], ff_chunk):
        u = jnp.dot(h, wu_ref[:, c0:c0 + ff_chunk], preferred_element_type=F32)
        a = jnp.square(jnp.maximum(u, 0.0)).astype(BF16)
        acc = acc + jnp.dot(a, wd_ref[c0:c0 + ff_chunk, :], preferred_element_type=F32)
    o_ref[...] = _rms(acc, fw_ref[...]) if final_norm else acc


def _merge_ffn(x2, proj, ydn, yatt, mem2, mem_norm_w, w_kv, w_branch, w_out, norm_w, w_up, w_down, final_w,
               final_norm, seq, m_len, xq_col0, tm):
    t, d = x2.shape
    bw = ydn.shape[1]
    dff = w_up.shape[1]
    n_pos = seq // tm
    xq_base = xq_col0 * LANES // bw
    resident = dict(pipeline_mode=pl.Buffered(1))
    return pl.pallas_call(
        functools.partial(_merge_ffn_kernel, ff_chunk=FFN_CHUNK, final_norm=final_norm, n_pos=n_pos),
        grid=(t // tm,),
        in_specs=[
            pl.BlockSpec((tm, d), lambda i: (i, 0)),
            pl.BlockSpec((tm, N_BRANCH * d), lambda i: (i, 0)),
            pl.BlockSpec((tm, bw), lambda i: (i, 0)),
            pl.BlockSpec((tm, bw), lambda i: (i, 0)),
            pl.BlockSpec((tm, bw), lambda i: (i, xq_base)),
            pl.BlockSpec((m_len, d), lambda i: (i // n_pos, 0)),
            pl.BlockSpec((1, d), lambda i: (0, 0)),
            pl.BlockSpec((d, 2 * bw), lambda i: (0, 0), **resident),
            pl.BlockSpec((N_BRANCH, bw, d), lambda i: (0, 0, 0), **resident),
            pl.BlockSpec((d, d), lambda i: (0, 0), **resident),
            pl.BlockSpec((1, d), lambda i: (0, 0)),
            pl.BlockSpec((d, dff), lambda i: (0, 0), **resident),
            pl.BlockSpec((dff, d), lambda i: (0, 0), **resident),
            pl.BlockSpec((1, d), lambda i: (0, 0)),
        ],
        out_specs=pl.BlockSpec((tm, d), lambda i: (i, 0)),
        out_shape=jax.ShapeDtypeStruct((t, d), F32),
        scratch_shapes=[pltpu.VMEM((m_len, bw), BF16), pltpu.VMEM((m_len, bw), BF16)],
        compiler_params=_params(("arbitrary",)),
        name="merge_ffn",
    )(x2, proj, ydn, yatt, proj, mem2, mem_norm_w, w_kv, w_branch, w_out, norm_w, w_up, w_down, final_w)


def _gate_lanes(beta_like, alpha_like):
    hp = DN_HEADS_PER_STEP
    lead = beta_like.shape[:-1]
    blocks = []
    for g in range(DN_GROUPS):
        pick = lambda a: [a[..., dd * DN_HEADS + g * hp: dd * DN_HEADS + (g + 1) * hp] for dd in range(2)]
        blocks += pick(beta_like) + [jnp.zeros(lead + (DN_ALPHA_LANE - 2 * hp,), beta_like.dtype)]
        blocks += pick(alpha_like) + [jnp.zeros(lead + (LANES - DN_ALPHA_LANE - 2 * hp,), beta_like.dtype)]
    return jnp.concatenate(blocks, axis=-1)


def _rope_tables(seq):
    pos = np.arange(seq)
    inv_freq = (ROPE_THETA ** (-np.arange(ROPE_PAIRS, dtype=np.float32) / ROPE_PAIRS)).astype(np.float32)
    ang_r = (pos // GRID_W).astype(np.float32)[:, None] * inv_freq
    ang_c = (pos % GRID_W).astype(np.float32)[:, None] * inv_freq
    ang = np.concatenate([ang_r, ang_r, ang_c, ang_c] * (LANES // ATT_HEAD_DIM), axis=1)
    return jnp.asarray(np.cos(ang), F32), jnp.asarray(np.sin(ang), F32)


def kernel(x, mem, mix_norm_w, w_in, dn_conv_w, dn_a_log, dn_dt_bias, dn_norm_w, q_norm_w, k_norm_w,
           mem_norm_w, w_mem_kv, w_branch, w_out, ffn_norm_w, w_up, w_down, final_norm_w):
    batch, seq, d = x.shape
    m_len = mem.shape[1]
    depth = w_in.shape[0]
    bw = d // 2
    dn_cols = 3 * bw
    assert seq % (2 * DN_BLOCK) == 0 and DN_HEADS * DN_HEAD_DIM == bw
    assert seq % INPROJ_ROWS == 0 and seq % ATT_Q_ROWS == 0 and seq % MERGE_ROWS == 0
    assert ATT_HEADS * ATT_HEAD_DIM == bw and X_HEADS * X_HEAD_DIM == bw

    n_gate = N_BRANCH * d
    col_dn = n_gate // LANES
    col_aq = col_dn + (dn_cols + bw) // LANES
    col_xq = col_aq + bw // LANES
    col_ak = col_xq + bw // LANES
    col_av = col_ak + 1
    n_proj = (col_av + 1) * LANES
    o_qkv, o_z, o_b, o_a = 0, dn_cols, dn_cols + bw, dn_cols + bw + 2 * DN_HEADS
    o_aq = o_a + 2 * DN_HEADS
    o_ak = o_aq + bw
    o_av = o_ak + ATT_KV_HEADS * ATT_HEAD_DIM
    o_xq = o_av + ATT_KV_HEADS * ATT_HEAD_DIM
    o_g = o_xq + bw

    cos_t, sin_t = _rope_tables(seq)
    x2 = x.reshape(batch * seq, d)
    mem2 = mem.reshape(batch * m_len, d)
    tile = lambda w: jnp.tile(w, LANES // ATT_HEAD_DIM)[None, :]

    for l in range(depth):
        wt = jnp.swapaxes(w_in[l], 0, 1).astype(BF16)
        w_prep = jnp.concatenate(
            [wt[o_g:], wt[o_qkv:o_b], wt[o_aq:o_ak], wt[o_xq:o_g], wt[o_ak:o_xq],
             _gate_lanes(wt[o_b:o_a].T, wt[o_a:o_aq].T).T], axis=0)
        zeros8 = jnp.zeros((2 * DN_HEADS,), F32)
        gate_rows = lambda p: jnp.broadcast_to(
            _gate_lanes(zeros8, p.reshape(-1)).reshape(DN_GROUPS, 1, LANES)[:, :, :DN_GATE_ROWS, None],
            (DN_GROUPS, 1, DN_GATE_ROWS, LANES))
        gpar = jnp.concatenate([gate_rows(dn_a_log[l]), gate_rows(dn_dt_bias[l])], axis=1)

        proj, ba = _inproj(x2, mix_norm_w[l][None, :], w_prep, cos_t, sin_t, tile(q_norm_w[l]), tile(k_norm_w[l]),
                           n_proj, q_cols=(col_aq * LANES, col_xq * LANES), k_cols=(col_ak * LANES, col_av * LANES),
                           tm=INPROJ_ROWS)
        ydn = _deltanet(proj, ba, dn_conv_w[l], gpar, dn_norm_w[l][None, :], batch, seq, col_dn)
        yatt = _attention(proj, batch, seq, col_aq, col_ak, col_av, tq=ATT_Q_ROWS)
        x2 = _merge_ffn(x2, proj, ydn, yatt, mem2, mem_norm_w[l][None, :], w_mem_kv[l].astype(BF16),
                        w_branch[l].astype(BF16), w_out[l].astype(BF16),
                        ffn_norm_w[l][None, :], w_up[l].astype(BF16), w_down[l].astype(BF16),
                        final_norm_w[None, :], final_norm=(l == depth - 1), seq=seq, m_len=m_len, xq_col0=col_xq,
                        tm=MERGE_ROWS)
    return x2.reshape(batch, seq, d)
```
